```python
import jax
import jax.numpy as jnp
from jax import lax
import numpy as np

D_MODEL = 1024
BATCH = 4
SEQ = 4096
DEPTH = 4
DEC_BATCH = 32
DEC_SEQ = 1
PAST_LEN = 8192
PAGE_SIZE = 128

N_MIXERS = 3
ATTN_GROUPS = ((128, 1), (512, 4), (2048, 16))
N_GROUPS = len(ATTN_GROUPS)
HEAD_DIM = 64
N_HEADS = D_MODEL // HEAD_DIM
ATTN_SCALE = HEAD_DIM ** -0.5
ROPE_THETA = 10000.0
POOL_WINDOWS = (2, 4, 8, 16)
POOL_GROUP = D_MODEL // len(POOL_WINDOWS)
POOL_PAST = max(POOL_WINDOWS) - 1
CONV_W = 3
D_FF = ((8 * D_MODEL // 3 + 127) // 128) * 128
ALPHA = (2.0 * DEPTH) ** 0.25
BETA = (8.0 * DEPTH) ** -0.25
LN_EPS = 1e-5
N_ATTN = len(range(0, DEPTH, N_MIXERS))
N_POOL = len(range(1, DEPTH, N_MIXERS))
N_SCONV = len(range(2, DEPTH, N_MIXERS))
F32 = jnp.float32

kernel_name = "dilated_pool_shortconv_hybrid_step"


def layer_norm(x, g, b):
    xf = x.astype(F32)
    mu = xf.mean(-1, keepdims=True)
    var = jnp.square(xf - mu).mean(-1, keepdims=True)
    return ((xf - mu) * lax.rsqrt(var + LN_EPS) * g + b).astype(x.dtype)


def rope(x, pos):
    half = HEAD_DIM // 2
    inv = ROPE_THETA ** (-jnp.arange(half, dtype=F32) / half)
    ang = pos.astype(F32)[:, None] * inv[None, :]
    cos = jnp.cos(ang)[None, :, None, None, :]
    sin = jnp.sin(ang)[None, :, None, None, :]
    xf = x.astype(F32)
    x1, x2 = xf[..., :half], xf[..., half:]
    return jnp.concatenate([x1 * cos - x2 * sin, x2 * cos + x1 * sin], -1).astype(x.dtype)


def causal_dwconv(z, prev, k):
    zp = jnp.concatenate([prev.astype(z.dtype), z], axis=1)
    T = z.shape[1]
    y = zp[:, 0:T] * k[0]
    for j in range(1, CONV_W):
        y = y + zp[:, j:j + T] * k[j]
    return y, zp[:, -(CONV_W - 1):]


def attn_qkv(x, w_qkv, pos0):
    B, T, _ = x.shape
    qkv = (x @ w_qkv).reshape(B, T, N_GROUPS, 3, N_HEADS, HEAD_DIM)
    pos = pos0 + jnp.arange(T)
    return rope(qkv[:, :, :, 0], pos), rope(qkv[:, :, :, 1], pos), qkv[:, :, :, 2]


def dilated_attn_prompt(q, k, v, window, dil):
    B, S, H, Dh = q.shape
    L = window // dil
    n = S // dil
    nb = -(-n // L)
    n_pad = nb * L
    Z = B * dil

    def to_res(t):
        t = t.reshape(B, n, dil, H, Dh).transpose(0, 2, 1, 3, 4).reshape(Z, n, H, Dh)
        return jnp.pad(t, ((0, 0), (0, n_pad - n), (0, 0), (0, 0)))

    def band(t):
        tp = jnp.pad(t, ((0, 0), (L, 0), (0, 0), (0, 0))).reshape(Z, nb + 1, L, H, Dh)
        return jnp.concatenate([tp[:, :-1], tp[:, 1:]], axis=2)

    qb = to_res(q).reshape(Z, nb, L, H, Dh)
    kb = band(to_res(k))
    vb = band(to_res(v)).astype(F32)
    s = jnp.einsum('znqhd,znkhd->znhqk', qb, kb, preferred_element_type=F32) * ATTN_SCALE
    qi = jnp.arange(L)[:, None]
    km = jnp.arange(2 * L)[None, :]
    delta = qi + L - km
    key_idx = jnp.arange(nb)[:, None, None] * L - L + km[None]
    mask = (delta >= 0) & (delta <= L) & (key_idx >= 0)
    s = jnp.where(mask[None, :, None], s, -jnp.inf)
    lse = jax.nn.logsumexp(s, axis=-1)
    p = jnp.exp(s - lse[..., None])
    o = jnp.einsum('znhqk,znkhd->znqhd', p, vb)
    o = o.reshape(B, dil, n_pad, H, Dh)[:, :, :n].transpose(0, 2, 1, 3, 4).reshape(B, S, H, Dh)
    lse = lse.transpose(0, 1, 3, 2).reshape(B, dil, n_pad, H)[:, :, :n]
    lse = lse.transpose(0, 2, 1, 3).reshape(B, S, H)
    return o, lse


def dilated_attn_step(q, kv_new, buf, window, dil):
    Lb = buf.shape[1]
    T = q.shape[1]
    ctx = jnp.concatenate([buf.astype(kv_new.dtype), kv_new], axis=1)
    nk = window // dil + 1
    idx = Lb + jnp.arange(T)[:, None] - dil * jnp.arange(nk)[None, :]
    valid = idx >= 0
    g = jnp.take(ctx, jnp.maximum(idx, 0), axis=1)
    s = jnp.einsum('bthd,btkhd->bthk', q, g[:, :, :, 0], preferred_element_type=F32) * ATTN_SCALE
    s = jnp.where(valid[None, :, None, :], s, -jnp.inf)
    lse = jax.nn.logsumexp(s, axis=-1)
    p = jnp.exp(s - lse[..., None])
    o = jnp.einsum('bthk,btkhd->bthd', p, g[:, :, :, 1].astype(F32))
    return o, lse


def merge_groups(outs, lses):
    o = jnp.stack(outs, axis=2)
    w = jax.nn.softmax(jnp.stack(lses, axis=2), axis=2)
    B, T = o.shape[0], o.shape[1]
    return jnp.einsum('btghd,btgh->bthd', o, w).reshape(B, T, N_HEADS * HEAD_DIM)


def pool_mixer(x, prev, w_in, w_grp, scale, w_out, pos0):
    B, T, _ = x.shape
    u = x @ w_in
    ctx = jnp.concatenate([prev.astype(u.dtype), u], axis=1)
    csum = jnp.concatenate([jnp.zeros((B, 1, D_MODEL), F32), jnp.cumsum(ctx.astype(F32), axis=1)], axis=1)
    pos = pos0 + jnp.arange(T)
    means = []
    for gi, w in enumerate(POOL_WINDOWS):
        cs = csum[..., gi * POOL_GROUP:(gi + 1) * POOL_GROUP]
        win_sum = cs[:, POOL_PAST + 1:] - cs[:, POOL_PAST + 1 - w:POOL_PAST + 1 - w + T]
        cnt = jnp.minimum(w, pos + 1).astype(F32)[None, :, None]
        means.append(win_sum / cnt)
    pooled = jnp.concatenate(means, axis=-1) - u.astype(F32)
    z = jnp.einsum('btgc,gcd->btgd', pooled.reshape(B, T, len(POOL_WINDOWS), POOL_GROUP), w_grp.astype(F32))
    z = (z.reshape(B, T, D_MODEL) * scale).astype(x.dtype)
    return z @ w_out, ctx[:, -POOL_PAST:]


def short_conv_mixer(x, prev, w_in, k, w_out):
    gb, gc, h = jnp.split(x @ w_in, 3, axis=-1)
    zc, new = causal_dwconv(gc * h, prev, k)
    return (gb * zc) @ w_out, new


def conv_ffn(x, prev, w_up, k, w_down):
    a, b = jnp.split(x @ w_up, 2, axis=-1)
    ac, new = causal_dwconv(a, prev, k)
    return (jax.nn.gelu(ac, approximate=False) * b) @ w_down, new


def trunk(x, pos0, kv_bufs, pool_st, sconv_st, ffn_st,
          attn_w_qkv, attn_w_o, pool_w_in, pool_w_grp, pool_scale, pool_w_out,
          sconv_w_in, sconv_k, sconv_w_out, ffn_w_up, ffn_k, ffn_w_down, ln_g, ln_b):
    B, T, _ = x.shape
    new_kv = [[] for _ in ATTN_GROUPS]
    new_pool, new_sconv, new_ffn = [], [], []
    ia = ib = ic = 0
    for i in range(DEPTH):
        kind = i % N_MIXERS
        if kind == 0:
            q, k, v = attn_qkv(x, attn_w_qkv[ia], pos0)
            outs, lses = [], []
            for g, (win, dil) in enumerate(ATTN_GROUPS):
                qg, kg, vg = q[:, :, g], k[:, :, g], v[:, :, g]
                kv_new = jnp.stack([kg, vg], axis=2)
                if kv_bufs is None:
                    o, l = dilated_attn_prompt(qg, kg, vg, win, dil)
                    new_kv[g].append(kv_new[:, -min(win, T):])
                else:
                    buf = kv_bufs[g][ia]
                    o, l = dilated_attn_step(qg, kv_new, buf, win, dil)
                    new_kv[g].append(jnp.concatenate([buf.astype(kv_new.dtype), kv_new], axis=1)[:, -buf.shape[1]:])
                outs.append(o)
                lses.append(l)
            mix = merge_groups(outs, lses).astype(x.dtype) @ attn_w_o[ia]
            ia += 1
        elif kind == 1:
            prev = jnp.zeros((B, POOL_PAST, D_MODEL), x.dtype) if pool_st is None else pool_st[ib]
            mix, st = pool_mixer(x, prev, pool_w_in[ib], pool_w_grp[ib], pool_scale[ib], pool_w_out[ib], pos0)
            new_pool.append(st)
            ib += 1
        else:
            prev = jnp.zeros((B, CONV_W - 1, D_MODEL), x.dtype) if sconv_st is None else sconv_st[ic]
            mix, st = short_conv_mixer(x, prev, sconv_w_in[ic], sconv_k[ic], sconv_w_out[ic])
            new_sconv.append(st)
            ic += 1
        x = layer_norm(ALPHA * x + mix, ln_g[i, 0], ln_b[i, 0])
        prev = jnp.zeros((B, CONV_W - 1, D_FF), x.dtype) if ffn_st is None else ffn_st[i]
        f, st = conv_ffn(x, prev, ffn_w_up[i], ffn_k[i], ffn_w_down[i])
        new_ffn.append(st)
        x = layer_norm(ALPHA * x + f, ln_g[i, 1], ln_b[i, 1])
    kv_out = [jnp.stack(lst) for lst in new_kv]
    return x, kv_out, jnp.stack(new_pool), jnp.stack(new_sconv), jnp.stack(new_ffn)


def setup_inputs(seed: int = 0) -> dict:
    key = jax.random.key(seed)
    ks = jax.random.split(key, 24)

    def nrm(k, shape, s=1.0):
        return jax.random.normal(k, shape, jnp.float32) * s

    qkv_cols = N_GROUPS * 3 * N_HEADS * HEAD_DIM
    kv_shape = lambda w: (N_ATTN, DEC_BATCH, min(w, PAST_LEN), 2, N_HEADS, HEAD_DIM)
    return {
        "x_prompt": nrm(ks[0], (BATCH, SEQ, D_MODEL)),
        "x_sample": nrm(ks[1], (DEC_BATCH, DEC_SEQ, D_MODEL)),
        "cache_kv_w128": nrm(ks[2], kv_shape(ATTN_GROUPS[0][0])),
        "cache_kv_w512": nrm(ks[3], kv_shape(ATTN_GROUPS[1][0])),
        "cache_kv_w2048": nrm(ks[4], kv_shape(ATTN_GROUPS[2][0])),
        "state_pool": nrm(ks[5], (N_POOL, DEC_BATCH, POOL_PAST, D_MODEL)),
        "state_sconv": nrm(ks[6], (N_SCONV, DEC_BATCH, CONV_W - 1, D_MODEL)),
        "state_ffn_conv": nrm(ks[7], (DEPTH, DEC_BATCH, CONV_W - 1, D_FF)),
        "attn_w_qkv": nrm(ks[8], (N_ATTN, D_MODEL, qkv_cols), D_MODEL ** -0.5),
        "attn_w_o": nrm(ks[9], (N_ATTN, N_HEADS * HEAD_DIM, D_MODEL), BETA * (N_HEADS * HEAD_DIM) ** -0.5),
        "pool_w_in": nrm(ks[10], (N_POOL, D_MODEL, D_MODEL), D_MODEL ** -0.5),
        "pool_w_grp": nrm(ks[11], (N_POOL, len(POOL_WINDOWS), POOL_GROUP, POOL_GROUP), POOL_GROUP ** -0.5),
        "pool_scale": 1.0 + nrm(ks[12], (N_POOL, D_MODEL), 0.02),
        "pool_w_out": nrm(ks[13], (N_POOL, D_MODEL, D_MODEL), BETA * D_MODEL ** -0.5),
        "sconv_w_in": nrm(ks[14], (N_SCONV, D_MODEL, 3 * D_MODEL), D_MODEL ** -0.5),
        "sconv_k": nrm(ks[15], (N_SCONV, CONV_W, D_MODEL), CONV_W ** -0.5),
        "sconv_w_out": nrm(ks[16], (N_SCONV, D_MODEL, D_MODEL), BETA * D_MODEL ** -0.5),
        "ffn_w_up": nrm(ks[17], (DEPTH, D_MODEL, 2 * D_FF), D_MODEL ** -0.5),
        "ffn_k": nrm(ks[18], (DEPTH, CONV_W, D_FF), CONV_W ** -0.5),
        "ffn_w_down": nrm(ks[19], (DEPTH, D_FF, D_MODEL), BETA * D_FF ** -0.5),
        "ln_g": 1.0 + nrm(ks[20], (DEPTH, 2, D_MODEL), 0.02),
        "ln_b": nrm(ks[21], (DEPTH, 2, D_MODEL), 0.02),
    }


def reference(x_prompt, x_sample, cache_kv_w128, cache_kv_w512, cache_kv_w2048,
              state_pool, state_sconv, state_ffn_conv,
              attn_w_qkv, attn_w_o, pool_w_in, pool_w_grp, pool_scale, pool_w_out,
              sconv_w_in, sconv_k, sconv_w_out, ffn_w_up, ffn_k, ffn_w_down, ln_g, ln_b):
    weights = (attn_w_qkv, attn_w_o, pool_w_in, pool_w_grp, pool_scale, pool_w_out,
               sconv_w_in, sconv_k, sconv_w_out, ffn_w_up, ffn_k, ffn_w_down, ln_g, ln_b)
    y_prompt, kv_p, pool_p, sconv_p, ffn_p = trunk(
        x_prompt, 0, None, None, None, None, *weights)
    y_sample, kv_s, pool_s, sconv_s, ffn_s = trunk(
        x_sample, PAST_LEN, (cache_kv_w128, cache_kv_w512, cache_kv_w2048),
        state_pool, state_sconv, state_ffn_conv, *weights)
    return (y_prompt, y_sample, kv_p[0], kv_s[0], kv_p[1], kv_s[1], kv_p[2], kv_s[2],
            pool_p, pool_s, sconv_p, sconv_s, ffn_p, ffn_s)
```

```python
import functools

import jax
import jax.numpy as jnp
from jax import lax
from jax.experimental import pallas as pl
from jax.experimental.pallas import tpu as pltpu

F32 = jnp.float32
BF16 = jnp.bfloat16

D_MODEL = 1024
DEPTH = 4
PAST_LEN = 8192
N_MIXERS = 3
ATTN_GROUPS = ((128, 1), (512, 4), (2048, 16))
N_GROUPS = len(ATTN_GROUPS)
HEAD_DIM = 64
N_HEADS = D_MODEL // HEAD_DIM
ATTN_SCALE = HEAD_DIM ** -0.5
ROPE_THETA = 10000.0
POOL_WINDOWS = (2, 4, 8, 16)
POOL_GROUP = D_MODEL // len(POOL_WINDOWS)
POOL_PAST = max(POOL_WINDOWS) - 1
CONV_W = 3
D_FF = ((8 * D_MODEL // 3 + 127) // 128) * 128
ALPHA = (2.0 * DEPTH) ** 0.25
LN_EPS = 1e-5
QKV_COLS = N_GROUPS * 3 * D_MODEL

LANES = 128
HALO = 16
ATTN_BLOCK = 128
FF_CHUNK = 256
VMEM_LIMIT = 56 * 1024 * 1024
NEG = -1e30


def _cparams(sem):
    return pltpu.CompilerParams(dimension_semantics=sem, vmem_limit_bytes=VMEM_LIMIT)


def _layer_norm(z, g, b):
    mu = jnp.mean(z, axis=-1, keepdims=True)
    zc = z - mu
    var = jnp.mean(zc * zc, axis=-1, keepdims=True)
    return zc * lax.rsqrt(var + LN_EPS) * g + b


def _mm(a, w):
    return jnp.dot(a, w, preferred_element_type=F32)


def _gelu(x):
    return 0.5 * x * (1.0 + lax.erf(x * (2.0 ** -0.5)))


def _const_spec(shape):
    nd = len(shape)
    return pl.BlockSpec(shape, lambda *_: (0,) * nd)


def _qkv_body(x_ref, w_ref, cos_ref, sin_ref, o_ref, xb_ref):
    j = pl.program_id(1)

    @pl.when(j == 0)
    def _():
        xb_ref[...] = x_ref[...].astype(BF16)

    y = _mm(xb_ref[...], w_ref[...])
    is_v = (j % 3) == 2

    @pl.when(is_v)
    def _():
        o_ref[...] = y

    @pl.when(jnp.logical_not(is_v))
    def _():
        cos = cos_ref[...]
        sin = sin_ref[...]
        lane = lax.broadcasted_iota(jnp.int32, cos.shape, 1)
        first_half = (lane & (HEAD_DIM // 2)) == 0
        for c in range(D_MODEL // LANES):
            cs = slice(c * LANES, (c + 1) * LANES)
            yc = y[:, cs]
            partner = jnp.where(first_half,
                                pltpu.roll(yc, LANES - HEAD_DIM // 2, 1),
                                pltpu.roll(yc, HEAD_DIM // 2, 1))
            o_ref[:, cs] = yc * cos + partner * sin


def _qkv_rope(x, w, cos, sin, tm, pos_tiles):
    m = x.shape[0]
    return pl.pallas_call(
        _qkv_body,
        out_shape=jax.ShapeDtypeStruct((m, QKV_COLS), F32),
        grid=(m // tm, QKV_COLS // D_MODEL),
        in_specs=[
            pl.BlockSpec((tm, D_MODEL), lambda i, j: (i, 0)),
            pl.BlockSpec((D_MODEL, D_MODEL), lambda i, j: (0, j)),
            pl.BlockSpec((tm, LANES), lambda i, j: (i % pos_tiles, 0)),
            pl.BlockSpec((tm, LANES), lambda i, j: (i % pos_tiles, 0)),
        ],
        out_specs=pl.BlockSpec((tm, D_MODEL), lambda i, j: (i, j)),
        scratch_shapes=[pltpu.VMEM((tm, D_MODEL), BF16)],
        compiler_params=_cparams(("arbitrary", "arbitrary")),
        name="qkv_rope",
    )(x, w, cos, sin)


def _rope_tables(pos):
    half = HEAD_DIM // 2
    inv = ROPE_THETA ** (-jnp.arange(half, dtype=F32) / half)
    ang = pos.astype(F32)[:, None] * inv[None, :]
    cos = jnp.cos(ang)
    sin = jnp.sin(ang)
    reps = LANES // HEAD_DIM
    cos_t = jnp.tile(jnp.concatenate([cos, cos], axis=1), (1, reps))
    sin_t = jnp.tile(jnp.concatenate([-sin, sin], axis=1), (1, reps))
    return cos_t, sin_t


def _attn_body(q_ref, kp_ref, kc_ref, vp_ref, vc_ref, o_ref, lse_ref):
    i = pl.program_id(2)
    tq = ATTN_BLOCK
    row = lax.broadcasted_iota(jnp.int32, (tq, 2 * tq), 0)
    col = lax.broadcasted_iota(jnp.int32, (tq, 2 * tq), 1)
    delta = row + tq - col
    valid = (delta >= 0) & (delta <= tq) & ((col >= tq) | (i > 0))
    lane = lax.broadcasted_iota(jnp.int32, (tq, LANES), 1)
    low = lane < HEAD_DIM
    lse_full = jnp.zeros((tq, LANES), F32)
    for hp in range(D_MODEL // LANES):
        cs = slice(hp * LANES, (hp + 1) * LANES)
        qp = q_ref[0, :, cs] * ATTN_SCALE
        k2 = jnp.concatenate([kp_ref[0, :, cs], kc_ref[0, :, cs]], axis=0).astype(BF16)
        v2 = jnp.concatenate([vp_ref[0, :, cs], vc_ref[0, :, cs]], axis=0).astype(BF16)
        outs = []
        for half in range(2):
            sel = low if half == 0 else jnp.logical_not(low)
            qm = jnp.where(sel, qp, 0.0).astype(BF16)
            s = lax.dot_general(qm, k2, (((1,), (1,)), ((), ())), preferred_element_type=F32)
            s = jnp.where(valid, s, NEG)
            m = jnp.max(s, axis=1, keepdims=True)
            p = jnp.exp(s - m)
            l = jnp.sum(p, axis=1, keepdims=True)
            pv = _mm(p.astype(BF16), v2)
            outs.append(pv * (1.0 / l))
            lse_full = jnp.where(lane == 2 * hp + half, m + jnp.log(l), lse_full)
        o_ref[0, :, cs] = jnp.where(low, outs[0], outs[1])
    lse_ref[0] = lse_full


def _attn_prompt(qkv, g, dil, batch, seq):
    n = seq // dil
    nb = n // ATTN_BLOCK
    units = QKV_COLS // D_MODEL
    qkv_v = qkv.reshape(batch, n, dil * QKV_COLS)
    blk = (1, ATTN_BLOCK, D_MODEL)

    def cur(unit):
        return pl.BlockSpec(blk, lambda b, r, i: (b, i, r * units + unit))

    def prev(unit):
        return pl.BlockSpec(blk, lambda b, r, i: (b, jnp.maximum(i - 1, 0), r * units + unit))

    o, lse = pl.pallas_call(
        _attn_body,
        out_shape=(jax.ShapeDtypeStruct((batch, n, dil * D_MODEL), F32),
                   jax.ShapeDtypeStruct((batch, n, dil * LANES), F32)),
        grid=(batch, dil, nb),
        in_specs=[cur(3 * g), prev(3 * g + 1), cur(3 * g + 1), prev(3 * g + 2), cur(3 * g + 2)],
        out_specs=(pl.BlockSpec(blk, lambda b, r, i: (b, i, r)),
                   pl.BlockSpec((1, ATTN_BLOCK, LANES), lambda b, r, i: (b, i, r))),
        compiler_params=_cparams(("arbitrary", "arbitrary", "arbitrary")),
        name=f"attn_prompt_g{g}",
    )(qkv_v, qkv_v, qkv_v, qkv_v, qkv_v)
    return o.reshape(batch * seq, D_MODEL), lse.reshape(batch * seq, LANES)


def _attn_step_body(q_ref, kn_ref, vn_ref, cache_ref, o_ref, lse_ref):
    rowh = lax.broadcasted_iota(jnp.int32, (N_HEADS, D_MODEL), 0)
    laneh = lax.broadcasted_iota(jnp.int32, (N_HEADS, D_MODEL), 1) // HEAD_DIM
    own = rowh == laneh
    q = q_ref[0] * ATTN_SCALE
    qbd = jnp.where(own, jnp.broadcast_to(q, (N_HEADS, D_MODEL)), 0.0).astype(BF16)
    k = cache_ref[:, 0:D_MODEL].astype(BF16)
    v = cache_ref[:, D_MODEL:2 * D_MODEL].astype(BF16)
    s_old = lax.dot_general(qbd, k, (((1,), (1,)), ((), ())), preferred_element_type=F32)
    kn = kn_ref[0].astype(BF16).astype(F32)
    vn = vn_ref[0].astype(BF16).astype(F32)
    s_new = jnp.sum(qbd.astype(F32) * kn, axis=1, keepdims=True)
    m = jnp.maximum(jnp.max(s_old, axis=1, keepdims=True), s_new)
    p_old = jnp.exp(s_old - m)
    p_new = jnp.exp(s_new - m)
    l = jnp.sum(p_old, axis=1, keepdims=True) + p_new
    pv = _mm(p_old.astype(BF16), v) + p_new.astype(BF16).astype(F32) * vn
    o16 = pv * (1.0 / l)
    o_ref[0] = jnp.sum(jnp.where(own, o16, 0.0), axis=0, keepdims=True)
    lse = m + jnp.log(l)
    r2 = lax.broadcasted_iota(jnp.int32, (N_HEADS, LANES), 0)
    l2 = lax.broadcasted_iota(jnp.int32, (N_HEADS, LANES), 1)
    lse_ref[0] = jnp.sum(jnp.where(r2 == l2, lse, 0.0), axis=0, keepdims=True)


def _attn_step(qkv3, cache, layer, g, dil):
    nb = cache.shape[1]
    win = cache.shape[2]
    cache_v = cache.reshape(cache.shape[0], nb, win // dil, dil * 2 * D_MODEL)
    row = (1, 1, D_MODEL)
    o, lse = pl.pallas_call(
        _attn_step_body,
        out_shape=(jax.ShapeDtypeStruct((nb, 1, D_MODEL), F32),
                   jax.ShapeDtypeStruct((nb, 1, LANES), F32)),
        grid=(nb,),
        in_specs=[
            pl.BlockSpec(row, lambda b: (b, 0, 3 * g)),
            pl.BlockSpec(row, lambda b: (b, 0, 3 * g + 1)),
            pl.BlockSpec(row, lambda b: (b, 0, 3 * g + 2)),
            pl.BlockSpec((None, None, ATTN_BLOCK, 2 * D_MODEL), lambda b: (layer, b, 0, 0)),
        ],
        out_specs=(pl.BlockSpec(row, lambda b: (b, 0, 0)),
                   pl.BlockSpec((1, 1, LANES), lambda b: (b, 0, 0))),
        compiler_params=_cparams(("arbitrary",)),
        name=f"attn_step_g{g}",
    )(qkv3, qkv3, qkv3, cache_v)
    return o.reshape(nb, D_MODEL), lse.reshape(nb, LANES)


def _merge_body(o0_ref, o1_ref, o2_ref, l0_ref, l1_ref, l2_ref, x_ref, wo_ref, e_ref,
                g_ref, b_ref, out_ref):
    ls = [l0_ref[...], l1_ref[...], l2_ref[...]]
    m = jnp.maximum(jnp.maximum(ls[0], ls[1]), ls[2])
    ex = [jnp.exp(l - m) for l in ls]
    inv_den = 1.0 / (ex[0] + ex[1] + ex[2])
    expand = e_ref[...]
    merged = None
    for ex_g, o_ref in zip(ex, (o0_ref, o1_ref, o2_ref)):
        w = ex_g * inv_den
        w_hi = w.astype(BF16)
        w_lo = (w - w_hi.astype(F32)).astype(BF16)
        wf = _mm(w_hi, expand) + _mm(w_lo, expand)
        t = wf * o_ref[...]
        merged = t if merged is None else merged + t
    mix = _mm(merged.astype(BF16), wo_ref[...])
    out_ref[...] = _layer_norm(ALPHA * x_ref[...] + mix, g_ref[...], b_ref[...])


def _merge_wo_ln(outs, lses, x, wo, ln_g, ln_b, tm):
    m = x.shape[0]
    head = jnp.arange(LANES)[:, None]
    lane_head = jnp.arange(D_MODEL)[None, :] // HEAD_DIM
    expand = (head == lane_head).astype(BF16)
    act = pl.BlockSpec((tm, D_MODEL), lambda i: (i, 0))
    lse = pl.BlockSpec((tm, LANES), lambda i: (i, 0))
    return pl.pallas_call(
        _merge_body,
        out_shape=jax.ShapeDtypeStruct((m, D_MODEL), F32),
        grid=(m // tm,),
        in_specs=[act, act, act, lse, lse, lse, act,
                  _const_spec((D_MODEL, D_MODEL)), _const_spec((LANES, D_MODEL)),
                  _const_spec((1, D_MODEL)), _const_spec((1, D_MODEL))],
        out_specs=act,
        compiler_params=_cparams(("arbitrary",)),
        name="merge_wo_ln",
    )(*outs, *lses, x, wo, expand, ln_g, ln_b)


def _load_halo(ext_ref, tm, tiles_per_seq):
    first = (pl.program_id(0) % tiles_per_seq) == 0

    @pl.when(first)
    def _():
        ext_ref[0:HALO, :] = jnp.zeros((HALO, ext_ref.shape[1]), F32)

    @pl.when(jnp.logical_not(first))
    def _():
        ext_ref[0:HALO, :] = ext_ref[tm:tm + HALO, :]


def _ffn_body(x_ref, wu_ref, k_ref, wd_ref, g_ref, b_ref, out_ref, st_ref,
              xb_ref, a_ref, acc_ref, *, tm, tiles_per_seq):
    _load_halo(a_ref, tm, tiles_per_seq)
    xb_ref[...] = x_ref[...].astype(BF16)
    for c in range(D_FF // FF_CHUNK):
        cs = slice(c * FF_CHUNK, (c + 1) * FF_CHUNK)
        gs = slice(D_FF + c * FF_CHUNK, D_FF + (c + 1) * FF_CHUNK)
        a = _mm(xb_ref[...], wu_ref[:, cs])
        gate = _mm(xb_ref[...], wu_ref[:, gs])
        a_ref[HALO:HALO + tm, cs] = a
        a1 = a_ref[HALO - 1:HALO - 1 + tm, cs]
        a2 = a_ref[HALO - 2:HALO - 2 + tm, cs]
        ac = a2 * k_ref[0:1, cs] + a1 * k_ref[1:2, cs] + a * k_ref[2:3, cs]
        hid = (_gelu(ac) * gate).astype(BF16)
        d = _mm(hid, wd_ref[cs, :])
        if c == 0:
            acc_ref[...] = d
        else:
            acc_ref[...] += d
    out_ref[...] = _layer_norm(ALPHA * x_ref[...] + acc_ref[...], g_ref[...], b_ref[...])
    st_ref[0] = a_ref[HALO + tm - (CONV_W - 1):HALO + tm, :]


def _ffn_prompt(x, wu, k, wd, ln_g, ln_b, batch, seq, tm):
    m = x.shape[0]
    tiles_per_seq = seq // tm
    act = pl.BlockSpec((tm, D_MODEL), lambda i: (i, 0))
    return pl.pallas_call(
        functools.partial(_ffn_body, tm=tm, tiles_per_seq=tiles_per_seq),
        out_shape=(jax.ShapeDtypeStruct((m, D_MODEL), F32),
                   jax.ShapeDtypeStruct((batch, CONV_W - 1, D_FF), F32)),
        grid=(m // tm,),
        in_specs=[act, _const_spec((D_MODEL, 2 * D_FF)), _const_spec((CONV_W, D_FF)),
                  _const_spec((D_FF, D_MODEL)), _const_spec((1, D_MODEL)), _const_spec((1, D_MODEL))],
        out_specs=(act, pl.BlockSpec((1, CONV_W - 1, D_FF), lambda i: (i // tiles_per_seq, 0, 0))),
        scratch_shapes=[pltpu.VMEM((tm, D_MODEL), BF16),
                        pltpu.VMEM((HALO + tm, D_FF), F32),
                        pltpu.VMEM((tm, D_MODEL), F32)],
        compiler_params=_cparams(("arbitrary",)),
        name="conv_ffn",
    )(x, wu, k, wd, ln_g, ln_b)


def _ffn_step_body(x_ref, st_ref, wu_ref, k_ref, wd_ref, g_ref, b_ref, out_ref, nst_ref):
    xb = x_ref[...].astype(BF16)
    a = _mm(xb, wu_ref[:, 0:D_FF])
    gate = _mm(xb, wu_ref[:, D_FF:2 * D_FF])
    a2 = st_ref[:, 0:D_FF]
    a1 = st_ref[:, D_FF:2 * D_FF]
    ac = a2 * k_ref[0:1, :] + a1 * k_ref[1:2, :] + a * k_ref[2:3, :]
    hid = (_gelu(ac) * gate).astype(BF16)
    f = _mm(hid, wd_ref[...])
    out_ref[...] = _layer_norm(ALPHA * x_ref[...] + f, g_ref[...], b_ref[...])
    nst_ref[:, 0:D_FF] = a1
    nst_ref[:, D_FF:2 * D_FF] = a


def _ffn_step(x, st, wu, k, wd, ln_g, ln_b):
    nb = x.shape[0]
    st2 = st.reshape(nb, (CONV_W - 1) * D_FF)
    shapes = [x.shape, st2.shape, wu.shape, k.shape, wd.shape, ln_g.shape, ln_b.shape]
    out, nst = pl.pallas_call(
        _ffn_step_body,
        out_shape=(jax.ShapeDtypeStruct(x.shape, F32), jax.ShapeDtypeStruct(st2.shape, F32)),
        grid=(1,),
        in_specs=[_const_spec(s) for s in shapes],
        out_specs=(_const_spec(x.shape), _const_spec(st2.shape)),
        compiler_params=_cparams(("arbitrary",)),
        name="conv_ffn_step",
    )(x, st2, wu, k, wd, ln_g, ln_b)
    return out, nst.reshape(st.shape)


def _pool_body(x_ref, win_ref, wg_ref, sc_ref, wout_ref, g_ref, b_ref, out_ref, st_ref,
               u_ref, *, tm, tiles_per_seq):
    _load_halo(u_ref, tm, tiles_per_seq)
    xb = x_ref[...].astype(BF16)
    u_ref[HALO:HALO + tm, :] = _mm(xb, win_ref[...])
    pos = (pl.program_id(0) % tiles_per_seq) * tm + lax.broadcasted_iota(jnp.int32, (tm, 1), 0)
    zs = []
    for gi, w in enumerate(POOL_WINDOWS):
        cs = slice(gi * POOL_GROUP, (gi + 1) * POOL_GROUP)
        u = u_ref[HALO:HALO + tm, cs]
        win_sum = u
        for j in range(1, w):
            win_sum = win_sum + u_ref[HALO - j:HALO - j + tm, cs]
        cnt = jnp.minimum(w, pos + 1).astype(F32)
        pooled = win_sum * (1.0 / cnt) - u
        z = _mm(pooled.astype(BF16), wg_ref[gi])
        zs.append(z * sc_ref[:, cs])
    zc = jnp.concatenate(zs, axis=1).astype(BF16)
    mix = _mm(zc, wout_ref[...])
    out_ref[...] = _layer_norm(ALPHA * x_ref[...] + mix, g_ref[...], b_ref[...])
    st_ref[0] = u_ref[HALO + tm - POOL_PAST:HALO + tm, :]


def _pool_prompt(x, win, wg, sc, wout, ln_g, ln_b, batch, seq, tm):
    m = x.shape[0]
    tiles_per_seq = seq // tm
    act = pl.BlockSpec((tm, D_MODEL), lambda i: (i, 0))
    return pl.pallas_call(
        functools.partial(_pool_body, tm=tm, tiles_per_seq=tiles_per_seq),
        out_shape=(jax.ShapeDtypeStruct((m, D_MODEL), F32),
                   jax.ShapeDtypeStruct((batch, POOL_PAST, D_MODEL), F32)),
        grid=(m // tm,),
        in_specs=[act, _const_spec(win.shape), _const_spec(wg.shape), _const_spec(sc.shape),
                  _const_spec(wout.shape), _const_spec((1, D_MODEL)), _const_spec((1, D_MODEL))],
        out_specs=(act, pl.BlockSpec((1, POOL_PAST, D_MODEL), lambda i: (i // tiles_per_seq, 0, 0))),
        scratch_shapes=[pltpu.VMEM((HALO + tm, D_MODEL), F32)],
        compiler_params=_cparams(("arbitrary",)),
        name="pool_mixer",
    )(x, win, wg, sc, wout, ln_g, ln_b)


def _pool_step_body(x_ref, st_ref, win_ref, wg_ref, sc_ref, wout_ref, g_ref, b_ref,
                    out_ref, nst_ref):
    xb = x_ref[...].astype(BF16)
    u_all = _mm(xb, win_ref[...])
    zs = []
    for gi, w in enumerate(POOL_WINDOWS):
        u = u_all[:, gi * POOL_GROUP:(gi + 1) * POOL_GROUP]
        win_sum = u
        for j in range(1, w):
            base = (POOL_PAST - j) * D_MODEL + gi * POOL_GROUP
            win_sum = win_sum + st_ref[:, base:base + POOL_GROUP]
        cnt = float(min(w, PAST_LEN + 1))
        pooled = win_sum * (1.0 / cnt) - u
        z = _mm(pooled.astype(BF16), wg_ref[gi])
        zs.append(z * sc_ref[:, gi * POOL_GROUP:(gi + 1) * POOL_GROUP])
    zc = jnp.concatenate(zs, axis=1).astype(BF16)
    mix = _mm(zc, wout_ref[...])
    out_ref[...] = _layer_norm(ALPHA * x_ref[...] + mix, g_ref[...], b_ref[...])
    keep = (POOL_PAST - 1) * D_MODEL
    nst_ref[:, 0:keep] = st_ref[:, D_MODEL:POOL_PAST * D_MODEL]
    nst_ref[:, keep:keep + D_MODEL] = u_all


def _pool_step(x, st, win, wg, sc, wout, ln_g, ln_b):
    nb = x.shape[0]
    st2 = st.reshape(nb, POOL_PAST * D_MODEL)
    ins = (x, st2, win, wg, sc, wout, ln_g, ln_b)
    out, nst = pl.pallas_call(
        _pool_step_body,
        out_shape=(jax.ShapeDtypeStruct(x.shape, F32), jax.ShapeDtypeStruct(st2.shape, F32)),
        grid=(1,),
        in_specs=[_const_spec(a.shape) for a in ins],
        out_specs=(_const_spec(x.shape), _const_spec(st2.shape)),
        compiler_params=_cparams(("arbitrary",)),
        name="pool_mixer_step",
    )(*ins)
    return out, nst.reshape(st.shape)


def _sconv_body(x_ref, win_ref, k_ref, wout_ref, g_ref, b_ref, out_ref, st_ref,
                z_ref, *, tm, tiles_per_seq):
    _load_halo(z_ref, tm, tiles_per_seq)
    xb = x_ref[...].astype(BF16)
    gb = _mm(xb, win_ref[:, 0:D_MODEL])
    gc = _mm(xb, win_ref[:, D_MODEL:2 * D_MODEL])
    hh = _mm(xb, win_ref[:, 2 * D_MODEL:3 * D_MODEL])
    z = gc * hh
    z_ref[HALO:HALO + tm, :] = z
    z1 = z_ref[HALO - 1:HALO - 1 + tm, :]
    z2 = z_ref[HALO - 2:HALO - 2 + tm, :]
    zc = z2 * k_ref[0:1, :] + z1 * k_ref[1:2, :] + z * k_ref[2:3, :]
    mix = _mm((gb * zc).astype(BF16), wout_ref[...])
    out_ref[...] = _layer_norm(ALPHA * x_ref[...] + mix, g_ref[...], b_ref[...])
    st_ref[0] = z_ref[HALO + tm - (CONV_W - 1):HALO + tm, :]


def _sconv_prompt(x, win, k, wout, ln_g, ln_b, batch, seq, tm):
    m = x.shape[0]
    tiles_per_seq = seq // tm
    act = pl.BlockSpec((tm, D_MODEL), lambda i: (i, 0))
    return pl.pallas_call(
        functools.partial(_sconv_body, tm=tm, tiles_per_seq=tiles_per_seq),
        out_shape=(jax.ShapeDtypeStruct((m, D_MODEL), F32),
                   jax.ShapeDtypeStruct((batch, CONV_W - 1, D_MODEL), F32)),
        grid=(m // tm,),
        in_specs=[act, _const_spec(win.shape), _const_spec(k.shape), _const_spec(wout.shape),
                  _const_spec((1, D_MODEL)), _const_spec((1, D_MODEL))],
        out_specs=(act, pl.BlockSpec((1, CONV_W - 1, D_MODEL), lambda i: (i // tiles_per_seq, 0, 0))),
        scratch_shapes=[pltpu.VMEM((HALO + tm, D_MODEL), F32)],
        compiler_params=_cparams(("arbitrary",)),
        name="sconv_mixer",
    )(x, win, k, wout, ln_g, ln_b)


def _sconv_step_body(x_ref, st_ref, win_ref, k_ref, wout_ref, g_ref, b_ref, out_ref, nst_ref):
    xb = x_ref[...].astype(BF16)
    gb = _mm(xb, win_ref[:, 0:D_MODEL])
    gc = _mm(xb, win_ref[:, D_MODEL:2 * D_MODEL])
    hh = _mm(xb, win_ref[:, 2 * D_MODEL:3 * D_MODEL])
    z = gc * hh
    z2 = st_ref[:, 0:D_MODEL]
    z1 = st_ref[:, D_MODEL:2 * D_MODEL]
    zc = z2 * k_ref[0:1, :] + z1 * k_ref[1:2, :] + z * k_ref[2:3, :]
    mix = _mm((gb * zc).astype(BF16), wout_ref[...])
    out_ref[...] = _layer_norm(ALPHA * x_ref[...] + mix, g_ref[...], b_ref[...])
    nst_ref[:, 0:D_MODEL] = z1
    nst_ref[:, D_MODEL:2 * D_MODEL] = z


def _sconv_step(x, st, win, k, wout, ln_g, ln_b):
    nb = x.shape[0]
    st2 = st.reshape(nb, (CONV_W - 1) * D_MODEL)
    ins = (x, st2, win, k, wout, ln_g, ln_b)
    out, nst = pl.pallas_call(
        _sconv_step_body,
        out_shape=(jax.ShapeDtypeStruct(x.shape, F32), jax.ShapeDtypeStruct(st2.shape, F32)),
        grid=(1,),
        in_specs=[_const_spec(a.shape) for a in ins],
        out_specs=(_const_spec(x.shape), _const_spec(st2.shape)),
        compiler_params=_cparams(("arbitrary",)),
        name="sconv_mixer_step",
    )(*ins)
    return out, nst.reshape(st.shape)


def kernel(x_prompt, x_sample, cache_kv_w128, cache_kv_w512, cache_kv_w2048, state_pool, state_sconv, state_ffn_conv, attn_w_qkv, attn_w_o, pool_w_in, pool_w_grp, pool_scale, pool_w_out, sconv_w_in, sconv_k, sconv_w_out, ffn_w_up, ffn_k, ffn_w_down, ln_g, ln_b):
    batch, seq, _ = x_prompt.shape
    nb = x_sample.shape[0]
    caches = (cache_kv_w128, cache_kv_w512, cache_kv_w2048)

    w_qkv = attn_w_qkv.astype(BF16)
    w_o = attn_w_o.astype(BF16)
    p_in = pool_w_in.astype(BF16)
    p_grp = pool_w_grp.astype(BF16)
    p_out = pool_w_out.astype(BF16)
    s_in = sconv_w_in.astype(BF16)
    s_out = sconv_w_out.astype(BF16)
    f_up = ffn_w_up.astype(BF16)
    f_down = ffn_w_down.astype(BF16)

    cos_p, sin_p = _rope_tables(jnp.arange(seq))
    cos_s, sin_s = _rope_tables(jnp.full((nb,), PAST_LEN))

    xp = x_prompt.reshape(batch * seq, D_MODEL)
    xs = x_sample.reshape(nb, D_MODEL)

    kv_p = [[] for _ in ATTN_GROUPS]
    kv_s = [[] for _ in ATTN_GROUPS]
    pool_p, pool_s, sconv_p, sconv_s, ffn_p, ffn_s = [], [], [], [], [], []
    ia = ib = ic = 0
    tm_qkv = 1024
    tm = 512
    for i in range(DEPTH):
        kind = i % N_MIXERS
        g1 = ln_g[i, 0].reshape(1, D_MODEL)
        b1 = ln_b[i, 0].reshape(1, D_MODEL)
        g2 = ln_g[i, 1].reshape(1, D_MODEL)
        b2 = ln_b[i, 1].reshape(1, D_MODEL)
        if kind == 0:
            qkv_p = _qkv_rope(xp, w_qkv[ia], cos_p, sin_p, tm_qkv, seq // tm_qkv)
            qkv_s = _qkv_rope(xs, w_qkv[ia], cos_s, sin_s, nb, 1)
            qkv_p3 = qkv_p.reshape(batch, seq, QKV_COLS)
            qkv_s3 = qkv_s.reshape(nb, 1, QKV_COLS)
            outs_p, lses_p, outs_s, lses_s = [], [], [], []
            for g, (win, dil) in enumerate(ATTN_GROUPS):
                o, l = _attn_prompt(qkv_p, g, dil, batch, seq)
                outs_p.append(o)
                lses_p.append(l)
                o, l = _attn_step(qkv_s3, caches[g], ia, g, dil)
                outs_s.append(o)
                lses_s.append(l)
                c0 = g * 3 * D_MODEL + D_MODEL
                kv_p[g].append(qkv_p3[:, seq - min(win, seq):, c0:c0 + 2 * D_MODEL]
                               .reshape(batch, min(win, seq), 2, N_HEADS, HEAD_DIM))
                kv_new = qkv_s3[:, :, c0:c0 + 2 * D_MODEL].reshape(nb, 1, 2, N_HEADS, HEAD_DIM)
                kv_s[g].append(jnp.concatenate([caches[g][ia][:, 1:], kv_new], axis=1))
            xp = _merge_wo_ln(outs_p, lses_p, xp, w_o[ia], g1, b1, tm)
            xs = _merge_wo_ln(outs_s, lses_s, xs, w_o[ia], g1, b1, nb)
            ia += 1
        elif kind == 1:
            sc = pool_scale[ib].reshape(1, D_MODEL)
            xp, st = _pool_prompt(xp, p_in[ib], p_grp[ib], sc, p_out[ib], g1, b1, batch, seq, tm)
            pool_p.append(st)
            xs, st = _pool_step(xs, state_pool[ib], p_in[ib], p_grp[ib], sc, p_out[ib], g1, b1)
            pool_s.append(st)
            ib += 1
        else:
            xp, st = _sconv_prompt(xp, s_in[ic], sconv_k[ic], s_out[ic], g1, b1, batch, seq, tm)
            sconv_p.append(st)
            xs, st = _sconv_step(xs, state_sconv[ic], s_in[ic], sconv_k[ic], s_out[ic], g1, b1)
            sconv_s.append(st)
            ic += 1
        xp, st = _ffn_prompt(xp, f_up[i], ffn_k[i], f_down[i], g2, b2, batch, seq, tm)
        ffn_p.append(st)
        xs, st = _ffn_step(xs, state_ffn_conv[i], f_up[i], ffn_k[i], f_down[i], g2, b2)
        ffn_s.append(st)

    y_prompt = xp.reshape(batch, seq, D_MODEL)
    y_sample = xs.reshape(nb, 1, D_MODEL)
    return (y_prompt, y_sample,
            jnp.stack(kv_p[0]), jnp.stack(kv_s[0]),
            jnp.stack(kv_p[1]), jnp.stack(kv_s[1]),
            jnp.stack(kv_p[2]), jnp.stack(kv_s[2]),
            jnp.stack(pool_p), jnp.stack(pool_s),
            jnp.stack(sconv_p), jnp.stack(sconv_s),
            jnp.stack(ffn_p), jnp.stack(ffn_s))
```

```python
import functools

import jax
import jax.numpy as jnp
from jax import lax
from jax.experimental import pallas as pl
from jax.experimental.pallas import tpu as pltpu

F32 = jnp.float32
BF16 = jnp.bfloat16

D_MODEL = 1024
DEPTH = 4
PAST_LEN = 8192
N_MIXERS = 3
ATTN_GROUPS = ((128, 1), (512, 4), (2048, 16))
N_GROUPS = len(ATTN_GROUPS)
HEAD_DIM = 64
N_HEADS = D_MODEL // HEAD_DIM
ATTN_SCALE = HEAD_DIM ** -0.5
ROPE_THETA = 10000.0
POOL_WINDOWS = (2, 4, 8, 16)
POOL_GROUP = D_MODEL // len(POOL_WINDOWS)
POOL_PAST = max(POOL_WINDOWS) - 1
CONV_W = 3
D_FF = ((8 * D_MODEL // 3 + 127) // 128) * 128
ALPHA = (2.0 * DEPTH) ** 0.25
LN_EPS = 1e-5
QKV_COLS = N_GROUPS * 3 * D_MODEL

LANES = 128
HALO = 16
ATTN_BLOCK = 128
FF_CHUNK = 256
CACHE_BLOCK_BYTES = 4 * 1024 * 1024
KV_TAIL_TILE = 512
VMEM_LIMIT = 56 * 1024 * 1024
NEG = -1e30


def _cparams(sem):
    return pltpu.CompilerParams(dimension_semantics=sem, vmem_limit_bytes=VMEM_LIMIT)


def _layer_norm(z, g, b):
    mu = jnp.mean(z, axis=-1, keepdims=True)
    zc = z - mu
    var = jnp.mean(zc * zc, axis=-1, keepdims=True)
    return zc * lax.rsqrt(var + LN_EPS) * g + b


def _mm(a, w):
    return jnp.dot(a, w, preferred_element_type=F32)


def _gelu(x):
    return 0.5 * x * (1.0 + lax.erf(x * (2.0 ** -0.5)))


def _const_spec(shape):
    nd = len(shape)
    return pl.BlockSpec(shape, lambda *_: (0,) * nd)


def _qkv_body(x_ref, w_ref, cos_ref, sin_ref, o_ref, xb_ref):
    j = pl.program_id(1)

    @pl.when(j == 0)
    def _():
        xb_ref[...] = x_ref[...].astype(BF16)

    y = _mm(xb_ref[...], w_ref[...])
    is_v = (j % 3) == 2

    @pl.when(is_v)
    def _():
        o_ref[...] = y

    @pl.when(jnp.logical_not(is_v))
    def _():
        cos = cos_ref[...]
        sin = sin_ref[...]
        lane = lax.broadcasted_iota(jnp.int32, cos.shape, 1)
        first_half = (lane & (HEAD_DIM // 2)) == 0
        for c in range(D_MODEL // LANES):
            cs = slice(c * LANES, (c + 1) * LANES)
            yc = y[:, cs]
            partner = jnp.where(first_half,
                                pltpu.roll(yc, LANES - HEAD_DIM // 2, 1),
                                pltpu.roll(yc, HEAD_DIM // 2, 1))
            o_ref[:, cs] = yc * cos + partner * sin


def _qkv_rope(x, w, cos, sin, tm, pos_tiles):
    m = x.shape[0]
    return pl.pallas_call(
        _qkv_body,
        out_shape=jax.ShapeDtypeStruct((m, QKV_COLS), F32),
        grid=(m // tm, QKV_COLS // D_MODEL),
        in_specs=[
            pl.BlockSpec((tm, D_MODEL), lambda i, j: (i, 0)),
            pl.BlockSpec((D_MODEL, D_MODEL), lambda i, j: (0, j)),
            pl.BlockSpec((tm, LANES), lambda i, j: (i % pos_tiles, 0)),
            pl.BlockSpec((tm, LANES), lambda i, j: (i % pos_tiles, 0)),
        ],
        out_specs=pl.BlockSpec((tm, D_MODEL), lambda i, j: (i, j)),
        scratch_shapes=[pltpu.VMEM((tm, D_MODEL), BF16)],
        compiler_params=_cparams(("arbitrary", "arbitrary")),
        name="qkv_rope",
    )(x, w, cos, sin)


def _rope_tables(pos):
    half = HEAD_DIM // 2
    inv = ROPE_THETA ** (-jnp.arange(half, dtype=F32) / half)
    ang = pos.astype(F32)[:, None] * inv[None, :]
    cos = jnp.cos(ang)
    sin = jnp.sin(ang)
    reps = LANES // HEAD_DIM
    cos_t = jnp.tile(jnp.concatenate([cos, cos], axis=1), (1, reps))
    sin_t = jnp.tile(jnp.concatenate([-sin, sin], axis=1), (1, reps))
    return cos_t, sin_t


def _band_mask(has_prev):
    tq = ATTN_BLOCK
    row = lax.broadcasted_iota(jnp.int32, (tq, 2 * tq), 0)
    col = lax.broadcasted_iota(jnp.int32, (tq, 2 * tq), 1)
    delta = row + tq - col
    return (delta >= 0) & (delta <= tq) & ((col >= tq) | has_prev)


def _attn_pair(qp, k2, v2, valid, low):
    k2 = k2.astype(BF16)
    v2 = v2.astype(BF16)
    outs, lses = [], []
    for half in range(2):
        sel = low if half == 0 else jnp.logical_not(low)
        qm = jnp.where(sel, qp * ATTN_SCALE, 0.0).astype(BF16)
        s = lax.dot_general(qm, k2, (((1,), (1,)), ((), ())), preferred_element_type=F32)
        s = jnp.where(valid, s, NEG)
        m = jnp.max(s, axis=1, keepdims=True)
        p = jnp.exp(s - m)
        l = jnp.sum(p, axis=1, keepdims=True)
        pv = _mm(p.astype(BF16), v2)
        outs.append(pv * (1.0 / l))
        lses.append(m + jnp.log(l))
    return jnp.where(low, outs[0], outs[1]), lses[0], lses[1]


def _attn_body(q_ref, kp_ref, kc_ref, vp_ref, vc_ref, o_ref, lse_ref):
    valid = _band_mask(pl.program_id(1) > 0)
    lane = lax.broadcasted_iota(jnp.int32, (ATTN_BLOCK, LANES), 1)
    low = lane < HEAD_DIM
    lse_full = jnp.zeros((ATTN_BLOCK, LANES), F32)
    for hp in range(D_MODEL // LANES):
        cs = slice(hp * LANES, (hp + 1) * LANES)
        k2 = jnp.concatenate([kp_ref[0, :, cs], kc_ref[0, :, cs]], axis=0)
        v2 = jnp.concatenate([vp_ref[0, :, cs], vc_ref[0, :, cs]], axis=0)
        o, lse0, lse1 = _attn_pair(q_ref[0, :, cs], k2, v2, valid, low)
        o_ref[0, :, cs] = o
        lse_full = jnp.where(lane == 2 * hp, lse0, lse_full)
        lse_full = jnp.where(lane == 2 * hp + 1, lse1, lse_full)
    lse_ref[0] = lse_full


def _attn_dilated_body(q_ref, kp_ref, kc_ref, vp_ref, vc_ref, o_ref, lse_ref, *, dil):
    hp = pl.program_id(2)
    valid = _band_mask(pl.program_id(1) > 0)
    lane = lax.broadcasted_iota(jnp.int32, (ATTN_BLOCK, LANES), 1)
    low = lane < HEAD_DIM

    @pl.when(hp == 0)
    def _():
        lse_ref[...] = jnp.zeros(lse_ref.shape, F32)

    def one_class(r, carry):
        rows = pl.ds(r, ATTN_BLOCK, stride=dil)
        k2 = jnp.concatenate([kp_ref[0, rows, :], kc_ref[0, rows, :]], axis=0)
        v2 = jnp.concatenate([vp_ref[0, rows, :], vc_ref[0, rows, :]], axis=0)
        o, lse0, lse1 = _attn_pair(q_ref[0, rows, :], k2, v2, valid, low)
        o_ref[0, rows, :] = o
        acc = lse_ref[0, rows, :]
        acc = jnp.where(lane == 2 * hp, lse0, acc)
        lse_ref[0, rows, :] = jnp.where(lane == 2 * hp + 1, lse1, acc)
        return carry

    lax.fori_loop(0, dil, one_class, 0)


def _attn_prompt(qkv3, g, dil, batch, seq):
    out_shape = (jax.ShapeDtypeStruct((batch, seq, D_MODEL), F32),
                 jax.ShapeDtypeStruct((batch, seq, LANES), F32))
    if dil == 1:
        blk = (1, ATTN_BLOCK, D_MODEL)

        def cur(unit):
            return pl.BlockSpec(blk, lambda b, i: (b, i, unit))

        def prev(unit):
            return pl.BlockSpec(blk, lambda b, i: (b, jnp.maximum(i - 1, 0), unit))

        o, lse = pl.pallas_call(
            _attn_body,
            out_shape=out_shape,
            grid=(batch, seq // ATTN_BLOCK),
            in_specs=[cur(3 * g), prev(3 * g + 1), cur(3 * g + 1), prev(3 * g + 2), cur(3 * g + 2)],
            out_specs=(pl.BlockSpec(blk, lambda b, i: (b, i, 0)),
                       pl.BlockSpec((1, ATTN_BLOCK, LANES), lambda b, i: (b, i, 0))),
            compiler_params=_cparams(("arbitrary", "arbitrary")),
            name=f"attn_prompt_g{g}",
        )(qkv3, qkv3, qkv3, qkv3, qkv3)
    else:
        rows = ATTN_BLOCK * dil
        pairs = D_MODEL // LANES
        blk = (1, rows, LANES)

        def cur(unit):
            return pl.BlockSpec(blk, lambda b, i, hp: (b, i, unit * pairs + hp))

        def prev(unit):
            return pl.BlockSpec(blk, lambda b, i, hp: (b, jnp.maximum(i - 1, 0), unit * pairs + hp))

        o, lse = pl.pallas_call(
            functools.partial(_attn_dilated_body, dil=dil),
            out_shape=out_shape,
            grid=(batch, seq // rows, pairs),
            in_specs=[cur(3 * g), prev(3 * g + 1), cur(3 * g + 1), prev(3 * g + 2), cur(3 * g + 2)],
            out_specs=(pl.BlockSpec(blk, lambda b, i, hp: (b, i, hp)),
                       pl.BlockSpec(blk, lambda b, i, hp: (b, i, 0))),
            compiler_params=_cparams(("arbitrary", "arbitrary", "arbitrary")),
            name=f"attn_prompt_g{g}",
        )(qkv3, qkv3, qkv3, qkv3, qkv3)
    return o.reshape(batch * seq, D_MODEL), lse.reshape(batch * seq, LANES)


def _cache_step_body(*refs, heads, win, dil, aliased):
    if aliased:
        q_ref, kn_ref, vn_ref, c_ref, _, nc_ref, o_ref, lse_ref = refs
    else:
        q_ref, kn_ref, vn_ref, c_ref, nc_ref, o_ref, lse_ref = refs
    b = pl.program_id(1)

    @pl.when(b == 0)
    def _():
        o_ref[...] = jnp.zeros(o_ref.shape, F32)
        lse_ref[...] = jnp.zeros(lse_ref.shape, F32)

    pick = lax.broadcasted_iota(jnp.int32, q_ref.shape, 1) == b

    def column(ref):
        return jnp.sum(jnp.where(pick, ref[...], 0.0), axis=1, keepdims=True)

    qc = column(q_ref) * ATTN_SCALE
    knc = column(kn_ref)
    vnc = column(vn_ref)
    t = lax.broadcasted_iota(jnp.int32, (1, win), 1)
    is_key = (t & (dil - 1)) == 0
    last = lax.broadcasted_iota(jnp.int32, (HEAD_DIM, win), 1) == win - 1
    mine = lax.broadcasted_iota(jnp.int32, (HEAD_DIM, LANES), 1) == b
    for h in range(heads):
        rs = slice(h * HEAD_DIM, (h + 1) * HEAD_DIM)
        k = c_ref[0, h]
        v = c_ref[1, h]
        q = qc[rs]
        kn = knc[rs]
        vn = vnc[rs]
        s = jnp.where(is_key, jnp.sum(k * q, axis=0, keepdims=True), NEG)
        s_new = jnp.sum(kn * q, axis=0, keepdims=True)
        m = jnp.maximum(jnp.max(s, axis=1, keepdims=True), s_new)
        p = jnp.exp(s - m)
        p_new = jnp.exp(s_new - m)
        l = jnp.sum(p, axis=1, keepdims=True) + p_new
        o = (jnp.sum(v * p, axis=1, keepdims=True) + vn * p_new) * (1.0 / l)
        lse = m + jnp.log(l)
        o_ref[rs, :] = jnp.where(mine, o, o_ref[rs, :])
        lse_ref[rs, :] = jnp.where(mine, lse, lse_ref[rs, :])
        nc_ref[0, h] = jnp.where(last, kn, pltpu.roll(k, win - 1, 1))
        nc_ref[1, h] = jnp.where(last, vn, pltpu.roll(v, win - 1, 1))


def _cache_step(qkv_t, cache_t, prev_out, layer, g, dil):
    n_layers, nb, _, _, _, win = cache_t.shape
    heads = max(1, min(N_HEADS, CACHE_BLOCK_BYTES // (2 * HEAD_DIM * win * 4)))
    rows = heads * HEAD_DIM
    per_unit = D_MODEL // rows
    aliased = prev_out is not None

    def unit(u):
        return pl.BlockSpec((rows, nb), lambda hc, b: (u * per_unit + hc, 0))

    cblk = (None, None, 2, heads, HEAD_DIM, win)
    in_specs = [unit(3 * g), unit(3 * g + 1), unit(3 * g + 2),
                pl.BlockSpec(cblk, lambda hc, b: (layer, b, 0, hc, 0, 0))]
    args = [qkv_t, qkv_t, qkv_t, cache_t]
    if aliased:
        in_specs.append(pl.BlockSpec(memory_space=pl.ANY))
        args.append(prev_out)
    col = pl.BlockSpec((rows, LANES), lambda hc, b: (hc, 0))
    return pl.pallas_call(
        functools.partial(_cache_step_body, heads=heads, win=win, dil=dil, aliased=aliased),
        out_shape=(jax.ShapeDtypeStruct(cache_t.shape, F32),
                   jax.ShapeDtypeStruct((D_MODEL, LANES), F32),
                   jax.ShapeDtypeStruct((D_MODEL, LANES), F32)),
        grid=(N_HEADS // heads, nb),
        in_specs=in_specs,
        out_specs=(pl.BlockSpec(cblk, lambda hc, b: (layer, b, 0, hc, 0, 0)), col, col),
        input_output_aliases={4: 0} if aliased else {},
        compiler_params=_cparams(("arbitrary", "arbitrary")),
        name=f"cache_step_g{g}",
    )(*args)


def _kv_tail_body(*refs):
    x_ref, o_ref = refs[0], refs[-1]
    o_ref[...] = x_ref[0].T


def _kv_tail(qkv3, prev_out, layer, n_layers, g, win):
    batch, seq, _ = qkv3.shape
    tt = min(win, KV_TAIL_TILE)
    first = (seq - win) // tt
    aliased = prev_out is not None
    in_specs = [pl.BlockSpec((1, tt, D_MODEL), lambda b, kv, i: (b, first + i, 3 * g + 1 + kv))]
    args = [qkv3]
    if aliased:
        in_specs.append(pl.BlockSpec(memory_space=pl.ANY))
        args.append(prev_out)
    return pl.pallas_call(
        _kv_tail_body,
        out_shape=jax.ShapeDtypeStruct((n_layers, batch, 2, D_MODEL, win), F32),
        grid=(batch, 2, win // tt),
        in_specs=in_specs,
        out_specs=pl.BlockSpec((None, None, None, D_MODEL, tt), lambda b, kv, i: (layer, b, kv, 0, i)),
        input_output_aliases={1: 0} if aliased else {},
        compiler_params=_cparams(("arbitrary", "arbitrary", "arbitrary")),
        name=f"kv_tail_g{g}",
    )(*args)


def _merge_step_body(o0_ref, o1_ref, o2_ref, l0_ref, l1_ref, l2_ref, x_ref, wo_ref,
                     g_ref, b_ref, out_ref):
    ls = [l0_ref[...], l1_ref[...], l2_ref[...]]
    m = jnp.maximum(jnp.maximum(ls[0], ls[1]), ls[2])
    ex = [jnp.exp(l - m) for l in ls]
    inv_den = 1.0 / (ex[0] + ex[1] + ex[2])
    merged_t = (ex[0] * o0_ref[...] + ex[1] * o1_ref[...] + ex[2] * o2_ref[...]) * inv_den
    merged = merged_t.T.astype(BF16)
    mix = _mm(merged, wo_ref[...])
    nb = x_ref.shape[0]
    out_ref[...] = _layer_norm(ALPHA * x_ref[...] + mix[0:nb, :], g_ref[...], b_ref[...])


def _merge_step(outs, lses, x, wo, ln_g, ln_b):
    ins = (*outs, *lses, x, wo, ln_g, ln_b)
    return pl.pallas_call(
        _merge_step_body,
        out_shape=jax.ShapeDtypeStruct(x.shape, F32),
        grid=(1,),
        in_specs=[_const_spec(a.shape) for a in ins],
        out_specs=_const_spec(x.shape),
        compiler_params=_cparams(("arbitrary",)),
        name="merge_wo_ln_step",
    )(*ins)


def _merge_body(o0_ref, o1_ref, o2_ref, l0_ref, l1_ref, l2_ref, x_ref, wo_ref, e_ref,
                g_ref, b_ref, out_ref):
    ls = [l0_ref[...], l1_ref[...], l2_ref[...]]
    m = jnp.maximum(jnp.maximum(ls[0], ls[1]), ls[2])
    ex = [jnp.exp(l - m) for l in ls]
    inv_den = 1.0 / (ex[0] + ex[1] + ex[2])
    expand = e_ref[...]
    merged = None
    for ex_g, o_ref in zip(ex, (o0_ref, o1_ref, o2_ref)):
        w = ex_g * inv_den
        w_hi = w.astype(BF16)
        w_lo = (w - w_hi.astype(F32)).astype(BF16)
        wf = _mm(w_hi, expand) + _mm(w_lo, expand)
        t = wf * o_ref[...]
        merged = t if merged is None else merged + t
    mix = _mm(merged.astype(BF16), wo_ref[...])
    out_ref[...] = _layer_norm(ALPHA * x_ref[...] + mix, g_ref[...], b_ref[...])


def _merge_wo_ln(outs, lses, x, wo, ln_g, ln_b, tm):
    m = x.shape[0]
    head = jnp.arange(LANES)[:, None]
    lane_head = jnp.arange(D_MODEL)[None, :] // HEAD_DIM
    expand = (head == lane_head).astype(BF16)
    act = pl.BlockSpec((tm, D_MODEL), lambda i: (i, 0))
    lse = pl.BlockSpec((tm, LANES), lambda i: (i, 0))
    return pl.pallas_call(
        _merge_body,
        out_shape=jax.ShapeDtypeStruct((m, D_MODEL), F32),
        grid=(m // tm,),
        in_specs=[act, act, act, lse, lse, lse, act,
                  _const_spec((D_MODEL, D_MODEL)), _const_spec((LANES, D_MODEL)),
                  _const_spec((1, D_MODEL)), _const_spec((1, D_MODEL))],
        out_specs=act,
        compiler_params=_cparams(("arbitrary",)),
        name="merge_wo_ln",
    )(*outs, *lses, x, wo, expand, ln_g, ln_b)


def _load_halo(ext_ref, tm, tiles_per_seq):
    first = (pl.program_id(0) % tiles_per_seq) == 0

    @pl.when(first)
    def _():
        ext_ref[0:HALO, :] = jnp.zeros((HALO, ext_ref.shape[1]), F32)

    @pl.when(jnp.logical_not(first))
    def _():
        ext_ref[0:HALO, :] = ext_ref[tm:tm + HALO, :]


def _ffn_body(x_ref, wu_ref, k_ref, wd_ref, g_ref, b_ref, out_ref, st_ref,
              xb_ref, a_ref, acc_ref, *, tm, tiles_per_seq):
    _load_halo(a_ref, tm, tiles_per_seq)
    xb_ref[...] = x_ref[...].astype(BF16)
    for c in range(D_FF // FF_CHUNK):
        cs = slice(c * FF_CHUNK, (c + 1) * FF_CHUNK)
        gs = slice(D_FF + c * FF_CHUNK, D_FF + (c + 1) * FF_CHUNK)
        a = _mm(xb_ref[...], wu_ref[:, cs])
        gate = _mm(xb_ref[...], wu_ref[:, gs])
        a_ref[HALO:HALO + tm, cs] = a
        a1 = a_ref[HALO - 1:HALO - 1 + tm, cs]
        a2 = a_ref[HALO - 2:HALO - 2 + tm, cs]
        ac = a2 * k_ref[0:1, cs] + a1 * k_ref[1:2, cs] + a * k_ref[2:3, cs]
        hid = (_gelu(ac) * gate).astype(BF16)
        d = _mm(hid, wd_ref[cs, :])
        if c == 0:
            acc_ref[...] = d
        else:
            acc_ref[...] += d
    out_ref[...] = _layer_norm(ALPHA * x_ref[...] + acc_ref[...], g_ref[...], b_ref[...])
    st_ref[0] = a_ref[HALO + tm - (CONV_W - 1):HALO + tm, :]


def _ffn_prompt(x, wu, k, wd, ln_g, ln_b, batch, seq, tm):
    m = x.shape[0]
    tiles_per_seq = seq // tm
    act = pl.BlockSpec((tm, D_MODEL), lambda i: (i, 0))
    return pl.pallas_call(
        functools.partial(_ffn_body, tm=tm, tiles_per_seq=tiles_per_seq),
        out_shape=(jax.ShapeDtypeStruct((m, D_MODEL), F32),
                   jax.ShapeDtypeStruct((batch, CONV_W - 1, D_FF), F32)),
        grid=(m // tm,),
        in_specs=[act, _const_spec((D_MODEL, 2 * D_FF)), _const_spec((CONV_W, D_FF)),
                  _const_spec((D_FF, D_MODEL)), _const_spec((1, D_MODEL)), _const_spec((1, D_MODEL))],
        out_specs=(act, pl.BlockSpec((1, CONV_W - 1, D_FF), lambda i: (i // tiles_per_seq, 0, 0))),
        scratch_shapes=[pltpu.VMEM((tm, D_MODEL), BF16),
                        pltpu.VMEM((HALO + tm, D_FF), F32),
                        pltpu.VMEM((tm, D_MODEL), F32)],
        compiler_params=_cparams(("arbitrary",)),
        name="conv_ffn",
    )(x, wu, k, wd, ln_g, ln_b)


def _ffn_step_body(x_ref, st_ref, wu_ref, k_ref, wd_ref, g_ref, b_ref, out_ref, nst_ref):
    xb = x_ref[...].astype(BF16)
    a = _mm(xb, wu_ref[:, 0:D_FF])
    gate = _mm(xb, wu_ref[:, D_FF:2 * D_FF])
    a2 = st_ref[:, 0:D_FF]
    a1 = st_ref[:, D_FF:2 * D_FF]
    ac = a2 * k_ref[0:1, :] + a1 * k_ref[1:2, :] + a * k_ref[2:3, :]
    hid = (_gelu(ac) * gate).astype(BF16)
    f = _mm(hid, wd_ref[...])
    out_ref[...] = _layer_norm(ALPHA * x_ref[...] + f, g_ref[...], b_ref[...])
    nst_ref[:, 0:D_FF] = a1
    nst_ref[:, D_FF:2 * D_FF] = a


def _ffn_step(x, st, wu, k, wd, ln_g, ln_b):
    nb = x.shape[0]
    st2 = st.reshape(nb, (CONV_W - 1) * D_FF)
    shapes = [x.shape, st2.shape, wu.shape, k.shape, wd.shape, ln_g.shape, ln_b.shape]
    out, nst = pl.pallas_call(
        _ffn_step_body,
        out_shape=(jax.ShapeDtypeStruct(x.shape, F32), jax.ShapeDtypeStruct(st2.shape, F32)),
        grid=(1,),
        in_specs=[_const_spec(s) for s in shapes],
        out_specs=(_const_spec(x.shape), _const_spec(st2.shape)),
        compiler_params=_cparams(("arbitrary",)),
        name="conv_ffn_step",
    )(x, st2, wu, k, wd, ln_g, ln_b)
    return out, nst.reshape(st.shape)


def _pool_body(x_ref, win_ref, wg_ref, sc_ref, wout_ref, g_ref, b_ref, out_ref, st_ref,
               u_ref, *, tm, tiles_per_seq):
    _load_halo(u_ref, tm, tiles_per_seq)
    xb = x_ref[...].astype(BF16)
    u_ref[HALO:HALO + tm, :] = _mm(xb, win_ref[...])
    pos = (pl.program_id(0) % tiles_per_seq) * tm + lax.broadcasted_iota(jnp.int32, (tm, 1), 0)
    zs = []
    for gi, w in enumerate(POOL_WINDOWS):
        cs = slice(gi * POOL_GROUP, (gi + 1) * POOL_GROUP)
        u = u_ref[HALO:HALO + tm, cs]
        win_sum = u
        for j in range(1, w):
            win_sum = win_sum + u_ref[HALO - j:HALO - j + tm, cs]
        cnt = jnp.minimum(w, pos + 1).astype(F32)
        pooled = win_sum * (1.0 / cnt) - u
        z = _mm(pooled.astype(BF16), wg_ref[gi])
        zs.append(z * sc_ref[:, cs])
    zc = jnp.concatenate(zs, axis=1).astype(BF16)
    mix = _mm(zc, wout_ref[...])
    out_ref[...] = _layer_norm(ALPHA * x_ref[...] + mix, g_ref[...], b_ref[...])
    st_ref[0] = u_ref[HALO + tm - POOL_PAST:HALO + tm, :]


def _pool_prompt(x, win, wg, sc, wout, ln_g, ln_b, batch, seq, tm):
    m = x.shape[0]
    tiles_per_seq = seq // tm
    act = pl.BlockSpec((tm, D_MODEL), lambda i: (i, 0))
    return pl.pallas_call(
        functools.partial(_pool_body, tm=tm, tiles_per_seq=tiles_per_seq),
        out_shape=(jax.ShapeDtypeStruct((m, D_MODEL), F32),
                   jax.ShapeDtypeStruct((batch, POOL_PAST, D_MODEL), F32)),
        grid=(m // tm,),
        in_specs=[act, _const_spec(win.shape), _const_spec(wg.shape), _const_spec(sc.shape),
                  _const_spec(wout.shape), _const_spec((1, D_MODEL)), _const_spec((1, D_MODEL))],
        out_specs=(act, pl.BlockSpec((1, POOL_PAST, D_MODEL), lambda i: (i // tiles_per_seq, 0, 0))),
        scratch_shapes=[pltpu.VMEM((HALO + tm, D_MODEL), F32)],
        compiler_params=_cparams(("arbitrary",)),
        name="pool_mixer",
    )(x, win, wg, sc, wout, ln_g, ln_b)


def _pool_step_body(x_ref, st_ref, win_ref, wg_ref, sc_ref, wout_ref, g_ref, b_ref,
                    out_ref, nst_ref):
    xb = x_ref[...].astype(BF16)
    u_all = _mm(xb, win_ref[...])
    zs = []
    for gi, w in enumerate(POOL_WINDOWS):
        u = u_all[:, gi * POOL_GROUP:(gi + 1) * POOL_GROUP]
        win_sum = u
        for j in range(1, w):
            base = (POOL_PAST - j) * D_MODEL + gi * POOL_GROUP
            win_sum = win_sum + st_ref[:, base:base + POOL_GROUP]
        cnt = float(min(w, PAST_LEN + 1))
        pooled = win_sum * (1.0 / cnt) - u
        z = _mm(pooled.astype(BF16), wg_ref[gi])
        zs.append(z * sc_ref[:, gi * POOL_GROUP:(gi + 1) * POOL_GROUP])
    zc = jnp.concatenate(zs, axis=1).astype(BF16)
    mix = _mm(zc, wout_ref[...])
    out_ref[...] = _layer_norm(ALPHA * x_ref[...] + mix, g_ref[...], b_ref[...])
    keep = (POOL_PAST - 1) * D_MODEL
    nst_ref[:, 0:keep] = st_ref[:, D_MODEL:POOL_PAST * D_MODEL]
    nst_ref[:, keep:keep + D_MODEL] = u_all


def _pool_step(x, st, win, wg, sc, wout, ln_g, ln_b):
    nb = x.shape[0]
    st2 = st.reshape(nb, POOL_PAST * D_MODEL)
    ins = (x, st2, win, wg, sc, wout, ln_g, ln_b)
    out, nst = pl.pallas_call(
        _pool_step_body,
        out_shape=(jax.ShapeDtypeStruct(x.shape, F32), jax.ShapeDtypeStruct(st2.shape, F32)),
        grid=(1,),
        in_specs=[_const_spec(a.shape) for a in ins],
        out_specs=(_const_spec(x.shape), _const_spec(st2.shape)),
        compiler_params=_cparams(("arbitrary",)),
        name="pool_mixer_step",
    )(*ins)
    return out, nst.reshape(st.shape)


def _sconv_body(x_ref, win_ref, k_ref, wout_ref, g_ref, b_ref, out_ref, st_ref,
                z_ref, *, tm, tiles_per_seq):
    _load_halo(z_ref, tm, tiles_per_seq)
    xb = x_ref[...].astype(BF16)
    gb = _mm(xb, win_ref[:, 0:D_MODEL])
    gc = _mm(xb, win_ref[:, D_MODEL:2 * D_MODEL])
    hh = _mm(xb, win_ref[:, 2 * D_MODEL:3 * D_MODEL])
    z = gc * hh
    z_ref[HALO:HALO + tm, :] = z
    z1 = z_ref[HALO - 1:HALO - 1 + tm, :]
    z2 = z_ref[HALO - 2:HALO - 2 + tm, :]
    zc = z2 * k_ref[0:1, :] + z1 * k_ref[1:2, :] + z * k_ref[2:3, :]
    mix = _mm((gb * zc).astype(BF16), wout_ref[...])
    out_ref[...] = _layer_norm(ALPHA * x_ref[...] + mix, g_ref[...], b_ref[...])
    st_ref[0] = z_ref[HALO + tm - (CONV_W - 1):HALO + tm, :]


def _sconv_prompt(x, win, k, wout, ln_g, ln_b, batch, seq, tm):
    m = x.shape[0]
    tiles_per_seq = seq // tm
    act = pl.BlockSpec((tm, D_MODEL), lambda i: (i, 0))
    return pl.pallas_call(
        functools.partial(_sconv_body, tm=tm, tiles_per_seq=tiles_per_seq),
        out_shape=(jax.ShapeDtypeStruct((m, D_MODEL), F32),
                   jax.ShapeDtypeStruct((batch, CONV_W - 1, D_MODEL), F32)),
        grid=(m // tm,),
        in_specs=[act, _const_spec(win.shape), _const_spec(k.shape), _const_spec(wout.shape),
                  _const_spec((1, D_MODEL)), _const_spec((1, D_MODEL))],
        out_specs=(act, pl.BlockSpec((1, CONV_W - 1, D_MODEL), lambda i: (i // tiles_per_seq, 0, 0))),
        scratch_shapes=[pltpu.VMEM((HALO + tm, D_MODEL), F32)],
        compiler_params=_cparams(("arbitrary",)),
        name="sconv_mixer",
    )(x, win, k, wout, ln_g, ln_b)


def _sconv_step_body(x_ref, st_ref, win_ref, k_ref, wout_ref, g_ref, b_ref, out_ref, nst_ref):
    xb = x_ref[...].astype(BF16)
    gb = _mm(xb, win_ref[:, 0:D_MODEL])
    gc = _mm(xb, win_ref[:, D_MODEL:2 * D_MODEL])
    hh = _mm(xb, win_ref[:, 2 * D_MODEL:3 * D_MODEL])
    z = gc * hh
    z2 = st_ref[:, 0:D_MODEL]
    z1 = st_ref[:, D_MODEL:2 * D_MODEL]
    zc = z2 * k_ref[0:1, :] + z1 * k_ref[1:2, :] + z * k_ref[2:3, :]
    mix = _mm((gb * zc).astype(BF16), wout_ref[...])
    out_ref[...] = _layer_norm(ALPHA * x_ref[...] + mix, g_ref[...], b_ref[...])
    nst_ref[:, 0:D_MODEL] = z1
    nst_ref[:, D_MODEL:2 * D_MODEL] = z


def _sconv_step(x, st, win, k, wout, ln_g, ln_b):
    nb = x.shape[0]
    st2 = st.reshape(nb, (CONV_W - 1) * D_MODEL)
    ins = (x, st2, win, k, wout, ln_g, ln_b)
    out, nst = pl.pallas_call(
        _sconv_step_body,
        out_shape=(jax.ShapeDtypeStruct(x.shape, F32), jax.ShapeDtypeStruct(st2.shape, F32)),
        grid=(1,),
        in_specs=[_const_spec(a.shape) for a in ins],
        out_specs=(_const_spec(x.shape), _const_spec(st2.shape)),
        compiler_params=_cparams(("arbitrary",)),
        name="sconv_mixer_step",
    )(*ins)
    return out, nst.reshape(st.shape)


def kernel(x_prompt, x_sample, cache_kv_w128, cache_kv_w512, cache_kv_w2048, state_pool, state_sconv, state_ffn_conv, attn_w_qkv, attn_w_o, pool_w_in, pool_w_grp, pool_scale, pool_w_out, sconv_w_in, sconv_k, sconv_w_out, ffn_w_up, ffn_k, ffn_w_down, ln_g, ln_b):
    batch, seq, _ = x_prompt.shape
    nb = x_sample.shape[0]
    caches = (cache_kv_w128, cache_kv_w512, cache_kv_w2048)

    w_qkv = attn_w_qkv.astype(BF16)
    w_o = attn_w_o.astype(BF16)
    p_in = pool_w_in.astype(BF16)
    p_grp = pool_w_grp.astype(BF16)
    p_out = pool_w_out.astype(BF16)
    s_in = sconv_w_in.astype(BF16)
    s_out = sconv_w_out.astype(BF16)
    f_up = ffn_w_up.astype(BF16)
    f_down = ffn_w_down.astype(BF16)

    cos_p, sin_p = _rope_tables(jnp.arange(seq))
    cos_s, sin_s = _rope_tables(jnp.full((nb,), PAST_LEN))

    xp = x_prompt.reshape(batch * seq, D_MODEL)
    xs = x_sample.reshape(nb, D_MODEL)

    caches_t = [c.transpose(0, 1, 3, 4, 5, 2) for c in caches]
    n_attn = caches[0].shape[0]
    kv_p = [None for _ in ATTN_GROUPS]
    kv_s = [None for _ in ATTN_GROUPS]
    pool_p, pool_s, sconv_p, sconv_s, ffn_p, ffn_s = [], [], [], [], [], []
    ia = ib = ic = 0
    tm_qkv = 1024
    tm = 512
    for i in range(DEPTH):
        kind = i % N_MIXERS
        g1 = ln_g[i, 0].reshape(1, D_MODEL)
        b1 = ln_b[i, 0].reshape(1, D_MODEL)
        g2 = ln_g[i, 1].reshape(1, D_MODEL)
        b2 = ln_b[i, 1].reshape(1, D_MODEL)
        if kind == 0:
            qkv_p = _qkv_rope(xp, w_qkv[ia], cos_p, sin_p, tm_qkv, seq // tm_qkv)
            qkv_s = _qkv_rope(xs, w_qkv[ia], cos_s, sin_s, nb, 1)
            qkv_p3 = qkv_p.reshape(batch, seq, QKV_COLS)
            qkv_st = qkv_s.T
            outs_p, lses_p, outs_s, lses_s = [], [], [], []
            for g, (win, dil) in enumerate(ATTN_GROUPS):
                o, l = _attn_prompt(qkv_p3, g, dil, batch, seq)
                outs_p.append(o)
                lses_p.append(l)
                kv_s[g], o, l = _cache_step(qkv_st, caches_t[g], kv_s[g], ia, g, dil)
                outs_s.append(o)
                lses_s.append(l)
                kv_p[g] = _kv_tail(qkv_p3, kv_p[g], ia, n_attn, g, min(win, seq))
            xp = _merge_wo_ln(outs_p, lses_p, xp, w_o[ia], g1, b1, tm)
            xs = _merge_step(outs_s, lses_s, xs, w_o[ia], g1, b1)
            ia += 1
        elif kind == 1:
            sc = pool_scale[ib].reshape(1, D_MODEL)
            xp, st = _pool_prompt(xp, p_in[ib], p_grp[ib], sc, p_out[ib], g1, b1, batch, seq, tm)
            pool_p.append(st)
            xs, st = _pool_step(xs, state_pool[ib], p_in[ib], p_grp[ib], sc, p_out[ib], g1, b1)
            pool_s.append(st)
            ib += 1
        else:
            xp, st = _sconv_prompt(xp, s_in[ic], sconv_k[ic], s_out[ic], g1, b1, batch, seq, tm)
            sconv_p.append(st)
            xs, st = _sconv_step(xs, state_sconv[ic], s_in[ic], sconv_k[ic], s_out[ic], g1, b1)
            sconv_s.append(st)
            ic += 1
        xp, st = _ffn_prompt(xp, f_up[i], ffn_k[i], f_down[i], g2, b2, batch, seq, tm)
        ffn_p.append(st)
        xs, st = _ffn_step(xs, state_ffn_conv[i], f_up[i], ffn_k[i], f_down[i], g2, b2)
        ffn_s.append(st)

    y_prompt = xp.reshape(batch, seq, D_MODEL)
    y_sample = xs.reshape(nb, 1, D_MODEL)

    def time_major(t, rows):
        t = t.reshape(n_attn, rows, 2, N_HEADS, HEAD_DIM, t.shape[-1])
        return t.transpose(0, 1, 5, 2, 3, 4)

    return (y_prompt, y_sample,
            time_major(kv_p[0], batch), time_major(kv_s[0], nb),
            time_major(kv_p[1], batch), time_major(kv_s[1], nb),
            time_major(kv_p[2], batch), time_major(kv_s[2], nb),
            jnp.stack(pool_p), jnp.stack(pool_s),
            jnp.stack(sconv_p), jnp.stack(sconv_s),
            jnp.stack(ffn_p), jnp.stack(ffn_s))
```

```python
import functools

import jax
import jax.numpy as jnp
from jax import lax
from jax.experimental import pallas as pl
from jax.experimental.pallas import tpu as pltpu

F32 = jnp.float32
BF16 = jnp.bfloat16

D_MODEL = 1024
DEPTH = 4
PAST_LEN = 8192
N_MIXERS = 3
ATTN_GROUPS = ((128, 1), (512, 4), (2048, 16))
N_GROUPS = len(ATTN_GROUPS)
HEAD_DIM = 64
N_HEADS = D_MODEL // HEAD_DIM
ATTN_SCALE = HEAD_DIM ** -0.5
ROPE_THETA = 10000.0
POOL_WINDOWS = (2, 4, 8, 16)
POOL_GROUP = D_MODEL // len(POOL_WINDOWS)
POOL_PAST = max(POOL_WINDOWS) - 1
CONV_W = 3
D_FF = ((8 * D_MODEL // 3 + 127) // 128) * 128
ALPHA = (2.0 * DEPTH) ** 0.25
LN_EPS = 1e-5
QKV_COLS = N_GROUPS * 3 * D_MODEL

LANES = 128
HALO = 16
ATTN_BLOCK = 128
FF_CHUNK = 256
CACHE_BLOCK_BYTES = 4 * 1024 * 1024
KV_TAIL_TILE = 512
QKV_ROW_CHUNK = 256
VMEM_LIMIT = 56 * 1024 * 1024
NEG = -1e30


def _cparams(sem):
    return pltpu.CompilerParams(dimension_semantics=sem, vmem_limit_bytes=VMEM_LIMIT)


def _layer_norm(z, g, b):
    mu = jnp.mean(z, axis=-1, keepdims=True)
    zc = z - mu
    var = jnp.mean(zc * zc, axis=-1, keepdims=True)
    return zc * lax.rsqrt(var + LN_EPS) * g + b


def _mm(a, w):
    return jnp.dot(a, w, preferred_element_type=F32)


def _gelu(x):
    return 0.5 * x * (1.0 + lax.erf(x * (2.0 ** -0.5)))


def _const_spec(shape):
    nd = len(shape)
    return pl.BlockSpec(shape, lambda *_: (0,) * nd)


def _qkv_body(x_ref, w_ref, cos_ref, sin_ref, o_ref, xb_ref, *, chunk):
    j = pl.program_id(1)

    @pl.when(j == 0)
    def _():
        xb_ref[...] = x_ref[...].astype(BF16)

    is_v = j == 2
    tm = x_ref.shape[0]
    lane = lax.broadcasted_iota(jnp.int32, (chunk, LANES), 1)
    first_half = (lane & (HEAD_DIM // 2)) == 0
    for rc in range(tm // chunk):
        rs = slice(rc * chunk, (rc + 1) * chunk)
        cos = jnp.where(is_v, 1.0, cos_ref[rs, :])
        sin = jnp.where(is_v, 0.0, sin_ref[rs, :])
        y = _mm(xb_ref[rs, :], w_ref[...])
        for c in range(D_MODEL // LANES):
            cs = slice(c * LANES, (c + 1) * LANES)
            yc = y[:, cs]
            partner = jnp.where(first_half,
                                pltpu.roll(yc, LANES - HEAD_DIM // 2, 1),
                                pltpu.roll(yc, HEAD_DIM // 2, 1))
            o_ref[rs, cs] = yc * cos + partner * sin


def _qkv_rope(x, w, g, cos, sin, tm, pos_tiles):
    m = x.shape[0]
    return pl.pallas_call(
        functools.partial(_qkv_body, chunk=min(tm, QKV_ROW_CHUNK)),
        out_shape=jax.ShapeDtypeStruct((m, 3 * D_MODEL), F32),
        grid=(m // tm, 3),
        in_specs=[
            pl.BlockSpec((tm, D_MODEL), lambda i, j: (i, 0)),
            pl.BlockSpec((D_MODEL, D_MODEL), lambda i, j: (0, 3 * g + j)),
            pl.BlockSpec((tm, LANES), lambda i, j: (i % pos_tiles, 0)),
            pl.BlockSpec((tm, LANES), lambda i, j: (i % pos_tiles, 0)),
        ],
        out_specs=pl.BlockSpec((tm, D_MODEL), lambda i, j: (i, j)),
        scratch_shapes=[pltpu.VMEM((tm, D_MODEL), BF16)],
        compiler_params=_cparams(("arbitrary", "arbitrary")),
        name=f"qkv_rope_g{g}",
    )(x, w, cos, sin)


def _store_classes(slab_ref, src_ref, dil):
    n = src_ref.shape[2]
    for r in range(dil):
        for c in range(src_ref.shape[3] // LANES):
            slab_ref[c, pl.ds(r, n, stride=dil), :] = src_ref[0, r, :, c * LANES:(c + 1) * LANES]


def _to_classes_body(x_ref, o_ref, slab_ref, *, dil):
    n = o_ref.shape[2]
    for c in range(D_MODEL // LANES):
        slab_ref[c] = x_ref[0, :, c * LANES:(c + 1) * LANES]
    for r in range(dil):
        for c in range(D_MODEL // LANES):
            o_ref[0, r, :, c * LANES:(c + 1) * LANES] = slab_ref[c, pl.ds(r, n, stride=dil), :]


def _to_classes(x3, dil, tm):
    batch, seq, _ = x3.shape
    return pl.pallas_call(
        functools.partial(_to_classes_body, dil=dil),
        out_shape=jax.ShapeDtypeStruct((batch, dil, seq // dil, D_MODEL), F32),
        grid=(batch, seq // tm),
        in_specs=[pl.BlockSpec((1, tm, D_MODEL), lambda b, i: (b, i, 0))],
        out_specs=pl.BlockSpec((1, dil, tm // dil, D_MODEL), lambda b, i: (b, 0, i, 0)),
        scratch_shapes=[pltpu.VMEM((D_MODEL // LANES, tm, LANES), F32)],
        compiler_params=_cparams(("arbitrary", "arbitrary")),
        name=f"to_classes_d{dil}",
    )(x3)


def _rope_tables(pos):
    half = HEAD_DIM // 2
    inv = ROPE_THETA ** (-jnp.arange(half, dtype=F32) / half)
    ang = pos.astype(F32)[:, None] * inv[None, :]
    cos = jnp.cos(ang)
    sin = jnp.sin(ang)
    reps = LANES // HEAD_DIM
    cos_t = jnp.tile(jnp.concatenate([cos, cos], axis=1), (1, reps))
    sin_t = jnp.tile(jnp.concatenate([-sin, sin], axis=1), (1, reps))
    return cos_t, sin_t


def _band_mask(has_prev):
    tq = ATTN_BLOCK
    row = lax.broadcasted_iota(jnp.int32, (tq, 2 * tq), 0)
    col = lax.broadcasted_iota(jnp.int32, (tq, 2 * tq), 1)
    delta = row + tq - col
    return (delta >= 0) & (delta <= tq) & ((col >= tq) | has_prev)


def _attn_pair(qp, k2, v2, valid, low):
    k2 = k2.astype(BF16)
    v2 = v2.astype(BF16)
    outs, lses = [], []
    for half in range(2):
        sel = low if half == 0 else jnp.logical_not(low)
        qm = jnp.where(sel, qp * ATTN_SCALE, 0.0).astype(BF16)
        s = lax.dot_general(qm, k2, (((1,), (1,)), ((), ())), preferred_element_type=F32)
        s = jnp.where(valid, s, NEG)
        m = jnp.max(s, axis=1, keepdims=True)
        p = jnp.exp(s - m)
        l = jnp.sum(p, axis=1, keepdims=True)
        pv = _mm(p.astype(BF16), v2)
        outs.append(pv * (1.0 / l))
        lses.append(m + jnp.log(l))
    return jnp.where(low, outs[0], outs[1]), lses[0], lses[1]


def _attn_body(q_ref, kp_ref, kc_ref, vp_ref, vc_ref, o_ref, lse_ref):
    valid = _band_mask(pl.program_id(1) > 0)
    lane = lax.broadcasted_iota(jnp.int32, (ATTN_BLOCK, LANES), 1)
    low = lane < HEAD_DIM
    lse_full = jnp.zeros((ATTN_BLOCK, LANES), F32)
    for hp in range(D_MODEL // LANES):
        cs = slice(hp * LANES, (hp + 1) * LANES)
        k2 = jnp.concatenate([kp_ref[0, :, cs], kc_ref[0, :, cs]], axis=0)
        v2 = jnp.concatenate([vp_ref[0, :, cs], vc_ref[0, :, cs]], axis=0)
        o, lse0, lse1 = _attn_pair(q_ref[0, :, cs], k2, v2, valid, low)
        o_ref[0, :, cs] = o
        lse_full = jnp.where(lane == 2 * hp, lse0, lse_full)
        lse_full = jnp.where(lane == 2 * hp + 1, lse1, lse_full)
    lse_ref[0] = lse_full


def _attn_prompt(qkv3, g):
    nseq, n, _ = qkv3.shape
    blk = (1, ATTN_BLOCK, D_MODEL)

    def cur(unit):
        return pl.BlockSpec(blk, lambda b, i: (b, i, unit))

    def prev(unit):
        return pl.BlockSpec(blk, lambda b, i: (b, jnp.maximum(i - 1, 0), unit))

    return pl.pallas_call(
        _attn_body,
        out_shape=(jax.ShapeDtypeStruct((nseq, n, D_MODEL), F32),
                   jax.ShapeDtypeStruct((nseq, n, LANES), F32)),
        grid=(nseq, n // ATTN_BLOCK),
        in_specs=[cur(0), prev(1), cur(1), prev(2), cur(2)],
        out_specs=(pl.BlockSpec(blk, lambda b, i: (b, i, 0)),
                   pl.BlockSpec((1, ATTN_BLOCK, LANES), lambda b, i: (b, i, 0))),
        compiler_params=_cparams(("arbitrary", "arbitrary")),
        name=f"attn_prompt_g{g}",
    )(qkv3, qkv3, qkv3, qkv3, qkv3)


def _cache_step_body(*refs, heads, win, dil, aliased):
    if aliased:
        q_ref, kn_ref, vn_ref, c_ref, _, nc_ref, o_ref, lse_ref = refs
    else:
        q_ref, kn_ref, vn_ref, c_ref, nc_ref, o_ref, lse_ref = refs
    b = pl.program_id(1)

    @pl.when(b == 0)
    def _():
        o_ref[...] = jnp.zeros(o_ref.shape, F32)
        lse_ref[...] = jnp.zeros(lse_ref.shape, F32)

    pick = lax.broadcasted_iota(jnp.int32, q_ref.shape, 1) == b

    def column(ref):
        return jnp.sum(jnp.where(pick, ref[...], 0.0), axis=1, keepdims=True)

    qc = column(q_ref) * ATTN_SCALE
    knc = column(kn_ref)
    vnc = column(vn_ref)
    t = lax.broadcasted_iota(jnp.int32, (1, win), 1)
    is_key = (t & (dil - 1)) == 0
    last = lax.broadcasted_iota(jnp.int32, (HEAD_DIM, win), 1) == win - 1
    mine = lax.broadcasted_iota(jnp.int32, (HEAD_DIM, LANES), 1) == b
    for h in range(heads):
        rs = slice(h * HEAD_DIM, (h + 1) * HEAD_DIM)
        k = c_ref[0, h]
        v = c_ref[1, h]
        q = qc[rs]
        kn = knc[rs]
        vn = vnc[rs]
        s = jnp.where(is_key, jnp.sum(k * q, axis=0, keepdims=True), NEG)
        s_new = jnp.sum(kn * q, axis=0, keepdims=True)
        m = jnp.maximum(jnp.max(s, axis=1, keepdims=True), s_new)
        p = jnp.exp(s - m)
        p_new = jnp.exp(s_new - m)
        l = jnp.sum(p, axis=1, keepdims=True) + p_new
        o = (jnp.sum(v * p, axis=1, keepdims=True) + vn * p_new) * (1.0 / l)
        lse = m + jnp.log(l)
        o_ref[rs, :] = jnp.where(mine, o, o_ref[rs, :])
        lse_ref[rs, :] = jnp.where(mine, lse, lse_ref[rs, :])
        nc_ref[0, h] = jnp.where(last, kn, pltpu.roll(k, win - 1, 1))
        nc_ref[1, h] = jnp.where(last, vn, pltpu.roll(v, win - 1, 1))


def _cache_step(qkv_t, cache_t, prev_out, layer, g, dil):
    n_layers, nb, _, _, _, win = cache_t.shape
    heads = max(1, min(N_HEADS, CACHE_BLOCK_BYTES // (2 * HEAD_DIM * win * 4)))
    rows = heads * HEAD_DIM
    per_unit = D_MODEL // rows
    aliased = prev_out is not None

    def unit(u):
        return pl.BlockSpec((rows, nb), lambda hc, b: (u * per_unit + hc, 0))

    cblk = (None, None, 2, heads, HEAD_DIM, win)
    in_specs = [unit(0), unit(1), unit(2),
                pl.BlockSpec(cblk, lambda hc, b: (layer, b, 0, hc, 0, 0))]
    args = [qkv_t, qkv_t, qkv_t, cache_t]
    if aliased:
        in_specs.append(pl.BlockSpec(memory_space=pl.ANY))
        args.append(prev_out)
    col = pl.BlockSpec((rows, LANES), lambda hc, b: (hc, 0))
    return pl.pallas_call(
        functools.partial(_cache_step_body, heads=heads, win=win, dil=dil, aliased=aliased),
        out_shape=(jax.ShapeDtypeStruct(cache_t.shape, F32),
                   jax.ShapeDtypeStruct((D_MODEL, LANES), F32),
                   jax.ShapeDtypeStruct((D_MODEL, LANES), F32)),
        grid=(N_HEADS // heads, nb),
        in_specs=in_specs,
        out_specs=(pl.BlockSpec(cblk, lambda hc, b: (layer, b, 0, hc, 0, 0)), col, col),
        input_output_aliases={4: 0} if aliased else {},
        compiler_params=_cparams(("arbitrary", "arbitrary")),
        name=f"cache_step_g{g}",
    )(*args)


def _kv_tail_body(*refs, dil, aliased):
    x_ref = refs[0]
    o_ref, slab_ref = refs[2:] if aliased else refs[1:]
    _store_classes(slab_ref, x_ref, dil)
    for c in range(D_MODEL // LANES):
        o_ref[c * LANES:(c + 1) * LANES, :] = slab_ref[c].T


def _kv_tail(qkv4, prev_out, layer, n_layers, g, win):
    batch, dil, n, _ = qkv4.shape
    tt = min(win, KV_TAIL_TILE)
    per_class = tt // dil
    first = (n - win // dil) // per_class
    aliased = prev_out is not None
    in_specs = [pl.BlockSpec((1, dil, per_class, D_MODEL), lambda b, kv, i: (b, 0, first + i, 1 + kv))]
    args = [qkv4]
    if aliased:
        in_specs.append(pl.BlockSpec(memory_space=pl.ANY))
        args.append(prev_out)
    return pl.pallas_call(
        functools.partial(_kv_tail_body, dil=dil, aliased=aliased),
        out_shape=jax.ShapeDtypeStruct((n_layers, batch, 2, D_MODEL, win), F32),
        grid=(batch, 2, win // tt),
        in_specs=in_specs,
        out_specs=pl.BlockSpec((None, None, None, D_MODEL, tt), lambda b, kv, i: (layer, b, kv, 0, i)),
        scratch_shapes=[pltpu.VMEM((D_MODEL // LANES, tt, LANES), F32)],
        input_output_aliases={1: 0} if aliased else {},
        compiler_params=_cparams(("arbitrary", "arbitrary", "arbitrary")),
        name=f"kv_tail_g{g}",
    )(*args)


def _merge_step_body(o0_ref, o1_ref, o2_ref, l0_ref, l1_ref, l2_ref, x_ref, wo_ref,
                     g_ref, b_ref, out_ref):
    ls = [l0_ref[...], l1_ref[...], l2_ref[...]]
    m = jnp.maximum(jnp.maximum(ls[0], ls[1]), ls[2])
    ex = [jnp.exp(l - m) for l in ls]
    inv_den = 1.0 / (ex[0] + ex[1] + ex[2])
    merged_t = (ex[0] * o0_ref[...] + ex[1] * o1_ref[...] + ex[2] * o2_ref[...]) * inv_den
    merged = merged_t.T.astype(BF16)
    mix = _mm(merged, wo_ref[...])
    nb = x_ref.shape[0]
    out_ref[...] = _layer_norm(ALPHA * x_ref[...] + mix[0:nb, :], g_ref[...], b_ref[...])


def _merge_step(outs, lses, x, wo, ln_g, ln_b):
    ins = (*outs, *lses, x, wo, ln_g, ln_b)
    return pl.pallas_call(
        _merge_step_body,
        out_shape=jax.ShapeDtypeStruct(x.shape, F32),
        grid=(1,),
        in_specs=[_const_spec(a.shape) for a in ins],
        out_specs=_const_spec(x.shape),
        compiler_params=_cparams(("arbitrary",)),
        name="merge_wo_ln_step",
    )(*ins)


def _merge_body(o0_ref, o1_ref, o2_ref, l0_ref, l1_ref, l2_ref, x_ref, wo_ref, e_ref,
                g_ref, b_ref, out_ref, oslab_ref, lslab_ref, merged_ref):
    o_refs = (o0_ref, o1_ref, o2_ref)
    l_refs = (l0_ref, l1_ref, l2_ref)
    for gi, (_, dil) in enumerate(ATTN_GROUPS):
        _store_classes(oslab_ref.at[gi], o_refs[gi], dil)
        _store_classes(lslab_ref.at[gi], l_refs[gi], dil)
    ls = [lslab_ref[gi, 0] for gi in range(N_GROUPS)]
    m = jnp.maximum(jnp.maximum(ls[0], ls[1]), ls[2])
    ex = [jnp.exp(l - m) for l in ls]
    inv_den = 1.0 / (ex[0] + ex[1] + ex[2])
    expand = e_ref[...]
    wfs = []
    for ex_g in ex:
        w = ex_g * inv_den
        w_hi = w.astype(BF16)
        w_lo = (w - w_hi.astype(F32)).astype(BF16)
        wfs.append(_mm(w_hi, expand) + _mm(w_lo, expand))
    for c in range(D_MODEL // LANES):
        cs = slice(c * LANES, (c + 1) * LANES)
        t = wfs[0][:, cs] * oslab_ref[0, c] + wfs[1][:, cs] * oslab_ref[1, c] + wfs[2][:, cs] * oslab_ref[2, c]
        merged_ref[:, cs] = t.astype(BF16)
    mix = _mm(merged_ref[...], wo_ref[...])
    out_ref[0] = _layer_norm(ALPHA * x_ref[0] + mix, g_ref[...], b_ref[...])


def _merge_wo_ln(outs, lses, x3, wo, ln_g, ln_b, tm):
    batch, seq, _ = x3.shape
    head = jnp.arange(LANES)[:, None]
    lane_head = jnp.arange(D_MODEL)[None, :] // HEAD_DIM
    expand = (head == lane_head).astype(BF16)
    act = pl.BlockSpec((1, tm, D_MODEL), lambda b, i: (b, i, 0))

    def classes(dil, width):
        return pl.BlockSpec((1, dil, tm // dil, width), lambda b, i: (b, 0, i, 0))

    pairs = D_MODEL // LANES
    return pl.pallas_call(
        _merge_body,
        out_shape=jax.ShapeDtypeStruct((batch, seq, D_MODEL), F32),
        grid=(batch, seq // tm),
        in_specs=[classes(dil, D_MODEL) for _, dil in ATTN_GROUPS]
        + [classes(dil, LANES) for _, dil in ATTN_GROUPS]
        + [act, _const_spec((D_MODEL, D_MODEL)), _const_spec((LANES, D_MODEL)),
           _const_spec((1, D_MODEL)), _const_spec((1, D_MODEL))],
        out_specs=act,
        scratch_shapes=[pltpu.VMEM((N_GROUPS, pairs, tm, LANES), F32),
                        pltpu.VMEM((N_GROUPS, 1, tm, LANES), F32),
                        pltpu.VMEM((tm, D_MODEL), BF16)],
        compiler_params=_cparams(("arbitrary", "arbitrary")),
        name="merge_wo_ln",
    )(*outs, *lses, x3, wo, expand, ln_g, ln_b)


def _load_halo(ext_ref, tm, tiles_per_seq):
    first = (pl.program_id(0) % tiles_per_seq) == 0

    @pl.when(first)
    def _():
        ext_ref[0:HALO, :] = jnp.zeros((HALO, ext_ref.shape[1]), F32)

    @pl.when(jnp.logical_not(first))
    def _():
        ext_ref[0:HALO, :] = ext_ref[tm:tm + HALO, :]


def _ffn_body(x_ref, wu_ref, k_ref, wd_ref, g_ref, b_ref, out_ref, st_ref,
              xb_ref, a_ref, acc_ref, *, tm, tiles_per_seq):
    _load_halo(a_ref, tm, tiles_per_seq)
    xb_ref[...] = x_ref[...].astype(BF16)
    for c in range(D_FF // FF_CHUNK):
        cs = slice(c * FF_CHUNK, (c + 1) * FF_CHUNK)
        gs = slice(D_FF + c * FF_CHUNK, D_FF + (c + 1) * FF_CHUNK)
        a = _mm(xb_ref[...], wu_ref[:, cs])
        gate = _mm(xb_ref[...], wu_ref[:, gs])
        a_ref[HALO:HALO + tm, cs] = a
        a1 = a_ref[HALO - 1:HALO - 1 + tm, cs]
        a2 = a_ref[HALO - 2:HALO - 2 + tm, cs]
        ac = a2 * k_ref[0:1, cs] + a1 * k_ref[1:2, cs] + a * k_ref[2:3, cs]
        hid = (_gelu(ac) * gate).astype(BF16)
        d = _mm(hid, wd_ref[cs, :])
        if c == 0:
            acc_ref[...] = d
        else:
            acc_ref[...] += d
    out_ref[...] = _layer_norm(ALPHA * x_ref[...] + acc_ref[...], g_ref[...], b_ref[...])
    st_ref[0] = a_ref[HALO + tm - (CONV_W - 1):HALO + tm, :]


def _ffn_prompt(x, wu, k, wd, ln_g, ln_b, batch, seq, tm):
    m = x.shape[0]
    tiles_per_seq = seq // tm
    act = pl.BlockSpec((tm, D_MODEL), lambda i: (i, 0))
    return pl.pallas_call(
        functools.partial(_ffn_body, tm=tm, tiles_per_seq=tiles_per_seq),
        out_shape=(jax.ShapeDtypeStruct((m, D_MODEL), F32),
                   jax.ShapeDtypeStruct((batch, CONV_W - 1, D_FF), F32)),
        grid=(m // tm,),
        in_specs=[act, _const_spec((D_MODEL, 2 * D_FF)), _const_spec((CONV_W, D_FF)),
                  _const_spec((D_FF, D_MODEL)), _const_spec((1, D_MODEL)), _const_spec((1, D_MODEL))],
        out_specs=(act, pl.BlockSpec((1, CONV_W - 1, D_FF), lambda i: (i // tiles_per_seq, 0, 0))),
        scratch_shapes=[pltpu.VMEM((tm, D_MODEL), BF16),
                        pltpu.VMEM((HALO + tm, D_FF), F32),
                        pltpu.VMEM((tm, D_MODEL), F32)],
        compiler_params=_cparams(("arbitrary",)),
        name="conv_ffn",
    )(x, wu, k, wd, ln_g, ln_b)


def _ffn_step_body(x_ref, st_ref, wu_ref, k_ref, wd_ref, g_ref, b_ref, out_ref, nst_ref):
    xb = x_ref[...].astype(BF16)
    a = _mm(xb, wu_ref[:, 0:D_FF])
    gate = _mm(xb, wu_ref[:, D_FF:2 * D_FF])
    a2 = st_ref[:, 0:D_FF]
    a1 = st_ref[:, D_FF:2 * D_FF]
    ac = a2 * k_ref[0:1, :] + a1 * k_ref[1:2, :] + a * k_ref[2:3, :]
    hid = (_gelu(ac) * gate).astype(BF16)
    f = _mm(hid, wd_ref[...])
    out_ref[...] = _layer_norm(ALPHA * x_ref[...] + f, g_ref[...], b_ref[...])
    nst_ref[:, 0:D_FF] = a1
    nst_ref[:, D_FF:2 * D_FF] = a


def _ffn_step(x, st, wu, k, wd, ln_g, ln_b):
    nb = x.shape[0]
    st2 = st.reshape(nb, (CONV_W - 1) * D_FF)
    shapes = [x.shape, st2.shape, wu.shape, k.shape, wd.shape, ln_g.shape, ln_b.shape]
    out, nst = pl.pallas_call(
        _ffn_step_body,
        out_shape=(jax.ShapeDtypeStruct(x.shape, F32), jax.ShapeDtypeStruct(st2.shape, F32)),
        grid=(1,),
        in_specs=[_const_spec(s) for s in shapes],
        out_specs=(_const_spec(x.shape), _const_spec(st2.shape)),
        compiler_params=_cparams(("arbitrary",)),
        name="conv_ffn_step",
    )(x, st2, wu, k, wd, ln_g, ln_b)
    return out, nst.reshape(st.shape)


def _pool_body(x_ref, win_ref, wg_ref, sc_ref, wout_ref, g_ref, b_ref, out_ref, st_ref,
               u_ref, *, tm, tiles_per_seq):
    _load_halo(u_ref, tm, tiles_per_seq)
    xb = x_ref[...].astype(BF16)
    u_ref[HALO:HALO + tm, :] = _mm(xb, win_ref[...])
    pos = (pl.program_id(0) % tiles_per_seq) * tm + lax.broadcasted_iota(jnp.int32, (tm, 1), 0)
    zs = []
    for gi, w in enumerate(POOL_WINDOWS):
        cs = slice(gi * POOL_GROUP, (gi + 1) * POOL_GROUP)
        u = u_ref[HALO:HALO + tm, cs]
        win_sum = u
        for j in range(1, w):
            win_sum = win_sum + u_ref[HALO - j:HALO - j + tm, cs]
        cnt = jnp.minimum(w, pos + 1).astype(F32)
        pooled = win_sum * (1.0 / cnt) - u
        z = _mm(pooled.astype(BF16), wg_ref[gi])
        zs.append(z * sc_ref[:, cs])
    zc = jnp.concatenate(zs, axis=1).astype(BF16)
    mix = _mm(zc, wout_ref[...])
    out_ref[...] = _layer_norm(ALPHA * x_ref[...] + mix, g_ref[...], b_ref[...])
    st_ref[0] = u_ref[HALO + tm - POOL_PAST:HALO + tm, :]


def _pool_prompt(x, win, wg, sc, wout, ln_g, ln_b, batch, seq, tm):
    m = x.shape[0]
    tiles_per_seq = seq // tm
    act = pl.BlockSpec((tm, D_MODEL), lambda i: (i, 0))
    return pl.pallas_call(
        functools.partial(_pool_body, tm=tm, tiles_per_seq=tiles_per_seq),
        out_shape=(jax.ShapeDtypeStruct((m, D_MODEL), F32),
                   jax.ShapeDtypeStruct((batch, POOL_PAST, D_MODEL), F32)),
        grid=(m // tm,),
        in_specs=[act, _const_spec(win.shape), _const_spec(wg.shape), _const_spec(sc.shape),
                  _const_spec(wout.shape), _const_spec((1, D_MODEL)), _const_spec((1, D_MODEL))],
        out_specs=(act, pl.BlockSpec((1, POOL_PAST, D_MODEL), lambda i: (i // tiles_per_seq, 0, 0))),
        scratch_shapes=[pltpu.VMEM((HALO + tm, D_MODEL), F32)],
        compiler_params=_cparams(("arbitrary",)),
        name="pool_mixer",
    )(x, win, wg, sc, wout, ln_g, ln_b)


def _pool_step_body(x_ref, st_ref, win_ref, wg_ref, sc_ref, wout_ref, g_ref, b_ref,
                    out_ref, nst_ref):
    xb = x_ref[...].astype(BF16)
    u_all = _mm(xb, win_ref[...])
    zs = []
    for gi, w in enumerate(POOL_WINDOWS):
        u = u_all[:, gi * POOL_GROUP:(gi + 1) * POOL_GROUP]
        win_sum = u
        for j in range(1, w):
            base = (POOL_PAST - j) * D_MODEL + gi * POOL_GROUP
            win_sum = win_sum + st_ref[:, base:base + POOL_GROUP]
        cnt = float(min(w, PAST_LEN + 1))
        pooled = win_sum * (1.0 / cnt) - u
        z = _mm(pooled.astype(BF16), wg_ref[gi])
        zs.append(z * sc_ref[:, gi * POOL_GROUP:(gi + 1) * POOL_GROUP])
    zc = jnp.concatenate(zs, axis=1).astype(BF16)
    mix = _mm(zc, wout_ref[...])
    out_ref[...] = _layer_norm(ALPHA * x_ref[...] + mix, g_ref[...], b_ref[...])
    keep = (POOL_PAST - 1) * D_MODEL
    nst_ref[:, 0:keep] = st_ref[:, D_MODEL:POOL_PAST * D_MODEL]
    nst_ref[:, keep:keep + D_MODEL] = u_all


def _pool_step(x, st, win, wg, sc, wout, ln_g, ln_b):
    nb = x.shape[0]
    st2 = st.reshape(nb, POOL_PAST * D_MODEL)
    ins = (x, st2, win, wg, sc, wout, ln_g, ln_b)
    out, nst = pl.pallas_call(
        _pool_step_body,
        out_shape=(jax.ShapeDtypeStruct(x.shape, F32), jax.ShapeDtypeStruct(st2.shape, F32)),
        grid=(1,),
        in_specs=[_const_spec(a.shape) for a in ins],
        out_specs=(_const_spec(x.shape), _const_spec(st2.shape)),
        compiler_params=_cparams(("arbitrary",)),
        name="pool_mixer_step",
    )(*ins)
    return out, nst.reshape(st.shape)


def _sconv_body(x_ref, win_ref, k_ref, wout_ref, g_ref, b_ref, out_ref, st_ref,
                z_ref, *, tm, tiles_per_seq):
    _load_halo(z_ref, tm, tiles_per_seq)
    xb = x_ref[...].astype(BF16)
    gb = _mm(xb, win_ref[:, 0:D_MODEL])
    gc = _mm(xb, win_ref[:, D_MODEL:2 * D_MODEL])
    hh = _mm(xb, win_ref[:, 2 * D_MODEL:3 * D_MODEL])
    z = gc * hh
    z_ref[HALO:HALO + tm, :] = z
    z1 = z_ref[HALO - 1:HALO - 1 + tm, :]
    z2 = z_ref[HALO - 2:HALO - 2 + tm, :]
    zc = z2 * k_ref[0:1, :] + z1 * k_ref[1:2, :] + z * k_ref[2:3, :]
    mix = _mm((gb * zc).astype(BF16), wout_ref[...])
    out_ref[...] = _layer_norm(ALPHA * x_ref[...] + mix, g_ref[...], b_ref[...])
    st_ref[0] = z_ref[HALO + tm - (CONV_W - 1):HALO + tm, :]


def _sconv_prompt(x, win, k, wout, ln_g, ln_b, batch, seq, tm):
    m = x.shape[0]
    tiles_per_seq = seq // tm
    act = pl.BlockSpec((tm, D_MODEL), lambda i: (i, 0))
    return pl.pallas_call(
        functools.partial(_sconv_body, tm=tm, tiles_per_seq=tiles_per_seq),
        out_shape=(jax.ShapeDtypeStruct((m, D_MODEL), F32),
                   jax.ShapeDtypeStruct((batch, CONV_W - 1, D_MODEL), F32)),
        grid=(m // tm,),
        in_specs=[act, _const_spec(win.shape), _const_spec(k.shape), _const_spec(wout.shape),
                  _const_spec((1, D_MODEL)), _const_spec((1, D_MODEL))],
        out_specs=(act, pl.BlockSpec((1, CONV_W - 1, D_MODEL), lambda i: (i // tiles_per_seq, 0, 0))),
        scratch_shapes=[pltpu.VMEM((HALO + tm, D_MODEL), F32)],
        compiler_params=_cparams(("arbitrary",)),
        name="sconv_mixer",
    )(x, win, k, wout, ln_g, ln_b)


def _sconv_step_body(x_ref, st_ref, win_ref, k_ref, wout_ref, g_ref, b_ref, out_ref, nst_ref):
    xb = x_ref[...].astype(BF16)
    gb = _mm(xb, win_ref[:, 0:D_MODEL])
    gc = _mm(xb, win_ref[:, D_MODEL:2 * D_MODEL])
    hh = _mm(xb, win_ref[:, 2 * D_MODEL:3 * D_MODEL])
    z = gc * hh
    z2 = st_ref[:, 0:D_MODEL]
    z1 = st_ref[:, D_MODEL:2 * D_MODEL]
    zc = z2 * k_ref[0:1, :] + z1 * k_ref[1:2, :] + z * k_ref[2:3, :]
    mix = _mm((gb * zc).astype(BF16), wout_ref[...])
    out_ref[...] = _layer_norm(ALPHA * x_ref[...] + mix, g_ref[...], b_ref[...])
    nst_ref[:, 0:D_MODEL] = z1
    nst_ref[:, D_MODEL:2 * D_MODEL] = z


def _sconv_step(x, st, win, k, wout, ln_g, ln_b):
    nb = x.shape[0]
    st2 = st.reshape(nb, (CONV_W - 1) * D_MODEL)
    ins = (x, st2, win, k, wout, ln_g, ln_b)
    out, nst = pl.pallas_call(
        _sconv_step_body,
        out_shape=(jax.ShapeDtypeStruct(x.shape, F32), jax.ShapeDtypeStruct(st2.shape, F32)),
        grid=(1,),
        in_specs=[_const_spec(a.shape) for a in ins],
        out_specs=(_const_spec(x.shape), _const_spec(st2.shape)),
        compiler_params=_cparams(("arbitrary",)),
        name="sconv_mixer_step",
    )(*ins)
    return out, nst.reshape(st.shape)


def kernel(x_prompt, x_sample, cache_kv_w128, cache_kv_w512, cache_kv_w2048, state_pool, state_sconv, state_ffn_conv, attn_w_qkv, attn_w_o, pool_w_in, pool_w_grp, pool_scale, pool_w_out, sconv_w_in, sconv_k, sconv_w_out, ffn_w_up, ffn_k, ffn_w_down, ln_g, ln_b):
    batch, seq, _ = x_prompt.shape
    nb = x_sample.shape[0]
    caches = (cache_kv_w128, cache_kv_w512, cache_kv_w2048)

    w_qkv = attn_w_qkv.astype(BF16)
    w_o = attn_w_o.astype(BF16)
    p_in = pool_w_in.astype(BF16)
    p_grp = pool_w_grp.astype(BF16)
    p_out = pool_w_out.astype(BF16)
    s_in = sconv_w_in.astype(BF16)
    s_out = sconv_w_out.astype(BF16)
    f_up = ffn_w_up.astype(BF16)
    f_down = ffn_w_down.astype(BF16)

    tables = [_rope_tables(jnp.arange(seq).reshape(seq // dil, dil).T.reshape(seq)) for _, dil in ATTN_GROUPS]
    cos_p = [t[0] for t in tables]
    sin_p = [t[1] for t in tables]
    cos_s, sin_s = _rope_tables(jnp.full((nb,), PAST_LEN))

    xp = x_prompt.reshape(batch * seq, D_MODEL)
    xs = x_sample.reshape(nb, D_MODEL)

    caches_t = [c.transpose(0, 1, 3, 4, 5, 2) for c in caches]
    n_attn = caches[0].shape[0]
    kv_p = [None for _ in ATTN_GROUPS]
    kv_s = [None for _ in ATTN_GROUPS]
    pool_p, pool_s, sconv_p, sconv_s, ffn_p, ffn_s = [], [], [], [], [], []
    ia = ib = ic = 0
    tm_qkv = 1024
    tm = 512
    for i in range(DEPTH):
        kind = i % N_MIXERS
        g1 = ln_g[i, 0].reshape(1, D_MODEL)
        b1 = ln_b[i, 0].reshape(1, D_MODEL)
        g2 = ln_g[i, 1].reshape(1, D_MODEL)
        b2 = ln_b[i, 1].reshape(1, D_MODEL)
        if kind == 0:
            x3 = xp.reshape(batch, seq, D_MODEL)
            outs_p, lses_p, outs_s, lses_s = [], [], [], []
            for g, (win, dil) in enumerate(ATTN_GROUPS):
                n = seq // dil
                xg = x3 if dil == 1 else _to_classes(x3, dil, tm_qkv)
                qkv_g = _qkv_rope(xg.reshape(batch * seq, D_MODEL), w_qkv[ia], g,
                                  cos_p[g], sin_p[g], tm_qkv, seq // tm_qkv)
                o, l = _attn_prompt(qkv_g.reshape(batch * dil, n, 3 * D_MODEL), g)
                outs_p.append(o.reshape(batch, dil, n, D_MODEL))
                lses_p.append(l.reshape(batch, dil, n, LANES))
                kv_p[g] = _kv_tail(qkv_g.reshape(batch, dil, n, 3 * D_MODEL), kv_p[g], ia, n_attn,
                                   g, min(win, seq))
                qkv_s = _qkv_rope(xs, w_qkv[ia], g, cos_s, sin_s, nb, 1)
                kv_s[g], o, l = _cache_step(qkv_s.T, caches_t[g], kv_s[g], ia, g, dil)
                outs_s.append(o)
                lses_s.append(l)
            xp = _merge_wo_ln(outs_p, lses_p, x3, w_o[ia], g1, b1, tm).reshape(batch * seq, D_MODEL)
            xs = _merge_step(outs_s, lses_s, xs, w_o[ia], g1, b1)
            ia += 1
        elif kind == 1:
            sc = pool_scale[ib].reshape(1, D_MODEL)
            xp, st = _pool_prompt(xp, p_in[ib], p_grp[ib], sc, p_out[ib], g1, b1, batch, seq, tm)
            pool_p.append(st)
            xs, st = _pool_step(xs, state_pool[ib], p_in[ib], p_grp[ib], sc, p_out[ib], g1, b1)
            pool_s.append(st)
            ib += 1
        else:
            xp, st = _sconv_prompt(xp, s_in[ic], sconv_k[ic], s_out[ic], g1, b1, batch, seq, tm)
            sconv_p.append(st)
            xs, st = _sconv_step(xs, state_sconv[ic], s_in[ic], sconv_k[ic], s_out[ic], g1, b1)
            sconv_s.append(st)
            ic += 1
        xp, st = _ffn_prompt(xp, f_up[i], ffn_k[i], f_down[i], g2, b2, batch, seq, tm)
        ffn_p.append(st)
        xs, st = _ffn_step(xs, state_ffn_conv[i], f_up[i], ffn_k[i], f_down[i], g2, b2)
        ffn_s.append(st)

    y_prompt = xp.reshape(batch, seq, D_MODEL)
    y_sample = xs.reshape(nb, 1, D_MODEL)

    def time_major(t, rows):
        t = t.reshape(n_attn, rows, 2, N_HEADS, HEAD_DIM, t.shape[-1])
        return t.transpose(0, 1, 5, 2, 3, 4)

    return (y_prompt, y_sample,
            time_major(kv_p[0], batch), time_major(kv_s[0], nb),
            time_major(kv_p[1], batch), time_major(kv_s[1], nb),
            time_major(kv_p[2], batch), time_major(kv_s[2], nb),
            jnp.stack(pool_p), jnp.stack(pool_s),
            jnp.stack(sconv_p), jnp.stack(sconv_s),
            jnp.stack(ffn_p), jnp.stack(ffn_s))
```

```python
import functools

import jax
import jax.numpy as jnp
from jax import lax
from jax.experimental import pallas as pl
from jax.experimental.pallas import tpu as pltpu

F32 = jnp.float32
BF16 = jnp.bfloat16

D_MODEL = 1024
DEPTH = 4
PAST_LEN = 8192
N_MIXERS = 3
ATTN_GROUPS = ((128, 1), (512, 4), (2048, 16))
N_GROUPS = len(ATTN_GROUPS)
HEAD_DIM = 64
N_HEADS = D_MODEL // HEAD_DIM
ATTN_SCALE = HEAD_DIM ** -0.5
ROPE_THETA = 10000.0
POOL_WINDOWS = (2, 4, 8, 16)
POOL_GROUP = D_MODEL // len(POOL_WINDOWS)
POOL_PAST = max(POOL_WINDOWS) - 1
CONV_W = 3
D_FF = ((8 * D_MODEL // 3 + 127) // 128) * 128
ALPHA = (2.0 * DEPTH) ** 0.25
LN_EPS = 1e-5
QKV_COLS = N_GROUPS * 3 * D_MODEL

LANES = 128
HALO = 16
ATTN_BLOCK = 128
FF_CHUNK = 256
CACHE_BLOCK_BYTES = 4 * 1024 * 1024
KV_TAIL_TILE = 512
QKV_ROW_CHUNK = 256
ATTN_STEP_ROWS = 512
VMEM_LIMIT = 56 * 1024 * 1024
NEG = -1e30


def _cparams(sem):
    return pltpu.CompilerParams(dimension_semantics=sem, vmem_limit_bytes=VMEM_LIMIT)


def _layer_norm(z, g, b):
    mu = jnp.mean(z, axis=-1, keepdims=True)
    zc = z - mu
    var = jnp.mean(zc * zc, axis=-1, keepdims=True)
    return zc * lax.rsqrt(var + LN_EPS) * g + b


def _mm(a, w):
    return jnp.dot(a, w, preferred_element_type=F32)


def _gelu(x):
    return 0.5 * x * (1.0 + lax.erf(x * (2.0 ** -0.5)))


def _const_spec(shape):
    nd = len(shape)
    return pl.BlockSpec(shape, lambda *_: (0,) * nd)


def _layer_spec(shape, layer):
    nd = len(shape)
    return pl.BlockSpec((None,) + tuple(shape), lambda *_: (layer,) + (0,) * nd)


def _qkv_body(x_ref, w_ref, cos_ref, sin_ref, o_ref, xb_ref, *, chunk):
    j = pl.program_id(1)

    @pl.when(j == 0)
    def _():
        xb_ref[...] = x_ref[...].astype(BF16)

    is_v = j == 2
    tm = x_ref.shape[0]
    lane = lax.broadcasted_iota(jnp.int32, (chunk, LANES), 1)
    first_half = (lane & (HEAD_DIM // 2)) == 0
    for rc in range(tm // chunk):
        rs = slice(rc * chunk, (rc + 1) * chunk)
        cos = jnp.where(is_v, 1.0, cos_ref[rs, :])
        sin = jnp.where(is_v, 0.0, sin_ref[rs, :])
        y = _mm(xb_ref[rs, :], w_ref[...])
        for c in range(D_MODEL // LANES):
            cs = slice(c * LANES, (c + 1) * LANES)
            yc = y[:, cs]
            partner = jnp.where(first_half,
                                pltpu.roll(yc, LANES - HEAD_DIM // 2, 1),
                                pltpu.roll(yc, HEAD_DIM // 2, 1))
            o_ref[rs, cs] = (yc * cos + partner * sin).astype(o_ref.dtype)


def _qkv_rope(x, w, layer, g, cos, sin, tm, pos_tiles, out_dtype):
    m = x.shape[0]
    return pl.pallas_call(
        functools.partial(_qkv_body, chunk=min(tm, QKV_ROW_CHUNK)),
        out_shape=jax.ShapeDtypeStruct((m, 3 * D_MODEL), out_dtype),
        grid=(m // tm, 3),
        in_specs=[
            pl.BlockSpec((tm, D_MODEL), lambda i, j: (i, 0)),
            pl.BlockSpec((None, D_MODEL, D_MODEL), lambda i, j: (layer, 0, 3 * g + j)),
            pl.BlockSpec((tm, LANES), lambda i, j: (i % pos_tiles, 0)),
            pl.BlockSpec((tm, LANES), lambda i, j: (i % pos_tiles, 0)),
        ],
        out_specs=pl.BlockSpec((tm, D_MODEL), lambda i, j: (i, j)),
        scratch_shapes=[pltpu.VMEM((tm, D_MODEL), BF16)],
        compiler_params=_cparams(("arbitrary", "arbitrary")),
        name=f"qkv_rope_g{g}",
    )(x, w, cos, sin)


def _store_classes(slab_ref, src_ref, dil):
    n = src_ref.shape[2]
    for r in range(dil):
        for c in range(src_ref.shape[3] // LANES):
            slab_ref[c, pl.ds(r, n, stride=dil), :] = src_ref[0, r, :, c * LANES:(c + 1) * LANES]


def _to_classes_body(x_ref, o_ref, slab_ref, *, dil):
    n = o_ref.shape[2]
    for c in range(D_MODEL // LANES):
        slab_ref[c] = x_ref[0, :, c * LANES:(c + 1) * LANES]
    for r in range(dil):
        for c in range(D_MODEL // LANES):
            o_ref[0, r, :, c * LANES:(c + 1) * LANES] = slab_ref[c, pl.ds(r, n, stride=dil), :]


def _to_classes(x3, dil, tm):
    batch, seq, _ = x3.shape
    return pl.pallas_call(
        functools.partial(_to_classes_body, dil=dil),
        out_shape=jax.ShapeDtypeStruct((batch, dil, seq // dil, D_MODEL), F32),
        grid=(batch, seq // tm),
        in_specs=[pl.BlockSpec((1, tm, D_MODEL), lambda b, i: (b, i, 0))],
        out_specs=pl.BlockSpec((1, dil, tm // dil, D_MODEL), lambda b, i: (b, 0, i, 0)),
        scratch_shapes=[pltpu.VMEM((D_MODEL // LANES, tm, LANES), F32)],
        compiler_params=_cparams(("arbitrary", "arbitrary")),
        name=f"to_classes_d{dil}",
    )(x3)


def _rope_tables(pos):
    half = HEAD_DIM // 2
    inv = ROPE_THETA ** (-jnp.arange(half, dtype=F32) / half)
    ang = pos.astype(F32)[:, None] * inv[None, :]
    cos = jnp.cos(ang)
    sin = jnp.sin(ang)
    reps = LANES // HEAD_DIM
    cos_t = jnp.tile(jnp.concatenate([cos, cos], axis=1), (1, reps))
    sin_t = jnp.tile(jnp.concatenate([-sin, sin], axis=1), (1, reps))
    return cos_t, sin_t


def _band_mask(has_prev):
    tq = ATTN_BLOCK
    row = lax.broadcasted_iota(jnp.int32, (tq, 2 * tq), 0)
    col = lax.broadcasted_iota(jnp.int32, (tq, 2 * tq), 1)
    delta = row + tq - col
    return (delta >= 0) & (delta <= tq) & ((col >= tq) | has_prev)


def _attn_pair(qp, k2, v2, valid, low):
    outs, lses = [], []
    for half in range(2):
        sel = low if half == 0 else jnp.logical_not(low)
        qm = jnp.where(sel, qp * ATTN_SCALE, 0.0)
        s = lax.dot_general(qm, k2, (((1,), (1,)), ((), ())), preferred_element_type=F32)
        s = jnp.where(valid, s, NEG)
        m = jnp.max(s, axis=1, keepdims=True)
        p = jnp.exp(s - m)
        l = jnp.sum(p, axis=1, keepdims=True)
        pv = _mm(p.astype(BF16), v2)
        outs.append(pv * (1.0 / l))
        lses.append(m + jnp.log(l))
    return jnp.where(low, outs[0], outs[1]), lses[0], lses[1]


def _attn_body(q_ref, kp_ref, kc_ref, vp_ref, vc_ref, o_ref, lse_ref, k_ref, v_ref):
    rows = q_ref.shape[1]
    not_first = pl.program_id(1) > 0
    k_ref[0:ATTN_BLOCK, :] = kp_ref[0]
    k_ref[ATTN_BLOCK:ATTN_BLOCK + rows, :] = kc_ref[0]
    v_ref[0:ATTN_BLOCK, :] = vp_ref[0]
    v_ref[ATTN_BLOCK:ATTN_BLOCK + rows, :] = vc_ref[0]
    lane = lax.broadcasted_iota(jnp.int32, (ATTN_BLOCK, LANES), 1)
    low = lane < HEAD_DIM
    for sb in range(rows // ATTN_BLOCK):
        valid = _band_mask(True if sb > 0 else not_first)
        qs = slice(sb * ATTN_BLOCK, (sb + 1) * ATTN_BLOCK)
        ks = slice(sb * ATTN_BLOCK, (sb + 2) * ATTN_BLOCK)
        lse_full = jnp.zeros((ATTN_BLOCK, LANES), F32)
        for hp in range(D_MODEL // LANES):
            cs = slice(hp * LANES, (hp + 1) * LANES)
            o, lse0, lse1 = _attn_pair(q_ref[0, qs, cs], k_ref[ks, cs], v_ref[ks, cs], valid, low)
            o_ref[0, qs, cs] = o
            lse_full = jnp.where(lane == 2 * hp, lse0, lse_full)
            lse_full = jnp.where(lane == 2 * hp + 1, lse1, lse_full)
        lse_ref[0, qs, :] = lse_full


def _attn_prompt(qkv3, g):
    nseq, n, _ = qkv3.shape
    rows = min(n, ATTN_STEP_ROWS)
    per_step = rows // ATTN_BLOCK
    blk = (1, rows, D_MODEL)

    def cur(unit):
        return pl.BlockSpec(blk, lambda b, i: (b, i, unit))

    def prev(unit):
        return pl.BlockSpec((1, ATTN_BLOCK, D_MODEL),
                            lambda b, i: (b, jnp.maximum(i * per_step - 1, 0), unit))

    return pl.pallas_call(
        _attn_body,
        out_shape=(jax.ShapeDtypeStruct((nseq, n, D_MODEL), F32),
                   jax.ShapeDtypeStruct((nseq, n, LANES), F32)),
        grid=(nseq, n // rows),
        in_specs=[cur(0), prev(1), cur(1), prev(2), cur(2)],
        out_specs=(pl.BlockSpec(blk, lambda b, i: (b, i, 0)),
                   pl.BlockSpec((1, rows, LANES), lambda b, i: (b, i, 0))),
        scratch_shapes=[pltpu.VMEM((ATTN_BLOCK + rows, D_MODEL), BF16),
                        pltpu.VMEM((ATTN_BLOCK + rows, D_MODEL), BF16)],
        compiler_params=_cparams(("arbitrary", "arbitrary")),
        name=f"attn_prompt_g{g}",
    )(qkv3, qkv3, qkv3, qkv3, qkv3)


def _cache_step_body(*refs, heads, win, dil, aliased):
    if aliased:
        q_ref, kn_ref, vn_ref, c_ref, _, nc_ref, o_ref, lse_ref = refs
    else:
        q_ref, kn_ref, vn_ref, c_ref, nc_ref, o_ref, lse_ref = refs
    b = pl.program_id(1)

    @pl.when(b == 0)
    def _():
        o_ref[...] = jnp.zeros(o_ref.shape, F32)
        lse_ref[...] = jnp.zeros(lse_ref.shape, F32)

    pick = lax.broadcasted_iota(jnp.int32, q_ref.shape, 1) == b

    def column(ref):
        return jnp.sum(jnp.where(pick, ref[...], 0.0), axis=1, keepdims=True)

    qc = column(q_ref) * ATTN_SCALE
    knc = column(kn_ref)
    vnc = column(vn_ref)
    t = lax.broadcasted_iota(jnp.int32, (1, win), 1)
    is_key = (t & (dil - 1)) == 0
    last = lax.broadcasted_iota(jnp.int32, (HEAD_DIM, win), 1) == win - 1
    mine = lax.broadcasted_iota(jnp.int32, (HEAD_DIM, LANES), 1) == b
    for h in range(heads):
        rs = slice(h * HEAD_DIM, (h + 1) * HEAD_DIM)
        k = c_ref[0, h]
        v = c_ref[1, h]
        q = qc[rs]
        kn = knc[rs]
        vn = vnc[rs]
        s = jnp.where(is_key, jnp.sum(k * q, axis=0, keepdims=True), NEG)
        s_new = jnp.sum(kn * q, axis=0, keepdims=True)
        m = jnp.maximum(jnp.max(s, axis=1, keepdims=True), s_new)
        p = jnp.exp(s - m)
        p_new = jnp.exp(s_new - m)
        l = jnp.sum(p, axis=1, keepdims=True) + p_new
        o = (jnp.sum(v * p, axis=1, keepdims=True) + vn * p_new) * (1.0 / l)
        lse = m + jnp.log(l)
        o_ref[rs, :] = jnp.where(mine, o, o_ref[rs, :])
        lse_ref[rs, :] = jnp.where(mine, lse, lse_ref[rs, :])
        nc_ref[0, h] = jnp.where(last, kn, pltpu.roll(k, win - 1, 1))
        nc_ref[1, h] = jnp.where(last, vn, pltpu.roll(v, win - 1, 1))


def _cache_step(qkv_t, cache_t, prev_out, layer, g, dil):
    n_layers, nb, _, _, _, win = cache_t.shape
    heads = max(1, min(N_HEADS, CACHE_BLOCK_BYTES // (2 * HEAD_DIM * win * 4)))
    rows = heads * HEAD_DIM
    per_unit = D_MODEL // rows
    aliased = prev_out is not None

    def unit(u):
        return pl.BlockSpec((rows, nb), lambda hc, b: (u * per_unit + hc, 0))

    cblk = (None, None, 2, heads, HEAD_DIM, win)
    in_specs = [unit(0), unit(1), unit(2),
                pl.BlockSpec(cblk, lambda hc, b: (layer, b, 0, hc, 0, 0))]
    args = [qkv_t, qkv_t, qkv_t, cache_t]
    if aliased:
        in_specs.append(pl.BlockSpec(memory_space=pl.ANY))
        args.append(prev_out)
    col = pl.BlockSpec((rows, LANES), lambda hc, b: (hc, 0))
    return pl.pallas_call(
        functools.partial(_cache_step_body, heads=heads, win=win, dil=dil, aliased=aliased),
        out_shape=(jax.ShapeDtypeStruct(cache_t.shape, F32),
                   jax.ShapeDtypeStruct((D_MODEL, LANES), F32),
                   jax.ShapeDtypeStruct((D_MODEL, LANES), F32)),
        grid=(N_HEADS // heads, nb),
        in_specs=in_specs,
        out_specs=(pl.BlockSpec(cblk, lambda hc, b: (layer, b, 0, hc, 0, 0)), col, col),
        input_output_aliases={4: 0} if aliased else {},
        compiler_params=_cparams(("arbitrary", "arbitrary")),
        name=f"cache_step_g{g}",
    )(*args)


def _kv_tail_body(*refs):
    x_ref, w_ref, cos_ref, sin_ref = refs[:4]
    o_ref = refs[-1]
    is_v = pl.program_id(2) == 1
    cos = jnp.where(is_v, 1.0, cos_ref[...])
    sin = jnp.where(is_v, 0.0, sin_ref[...])
    lane = lax.broadcasted_iota(jnp.int32, cos.shape, 1)
    first_half = (lane & (HEAD_DIM // 2)) == 0
    y = _mm(x_ref[0].astype(BF16), w_ref[...])
    for c in range(D_MODEL // LANES):
        cs = slice(c * LANES, (c + 1) * LANES)
        yc = y[:, cs]
        partner = jnp.where(first_half,
                            pltpu.roll(yc, LANES - HEAD_DIM // 2, 1),
                            pltpu.roll(yc, HEAD_DIM // 2, 1))
        o_ref[cs, :] = (yc * cos + partner * sin).T


def _kv_tail(x3, w, cos, sin, prev_out, layer, n_layers, g, win):
    batch, seq, _ = x3.shape
    tt = min(win, KV_TAIL_TILE)
    first = (seq - win) // tt
    aliased = prev_out is not None
    table = pl.BlockSpec((tt, LANES), lambda b, i, kv: (first + i, 0))
    in_specs = [pl.BlockSpec((1, tt, D_MODEL), lambda b, i, kv: (b, first + i, 0)),
                pl.BlockSpec((None, D_MODEL, D_MODEL), lambda b, i, kv: (layer, 0, 3 * g + 1 + kv)),
                table, table]
    args = [x3, w, cos, sin]
    if aliased:
        in_specs.append(pl.BlockSpec(memory_space=pl.ANY))
        args.append(prev_out)
    return pl.pallas_call(
        _kv_tail_body,
        out_shape=jax.ShapeDtypeStruct((n_layers, batch, 2, D_MODEL, win), F32),
        grid=(batch, win // tt, 2),
        in_specs=in_specs,
        out_specs=pl.BlockSpec((None, None, None, D_MODEL, tt), lambda b, i, kv: (layer, b, kv, 0, i)),
        input_output_aliases={4: 0} if aliased else {},
        compiler_params=_cparams(("arbitrary", "arbitrary", "arbitrary")),
        name=f"kv_tail_g{g}",
    )(*args)


def _merge_step_body(o0_ref, o1_ref, o2_ref, l0_ref, l1_ref, l2_ref, x_ref, wo_ref,
                     g_ref, b_ref, out_ref):
    ls = [l0_ref[...], l1_ref[...], l2_ref[...]]
    m = jnp.maximum(jnp.maximum(ls[0], ls[1]), ls[2])
    ex = [jnp.exp(l - m) for l in ls]
    inv_den = 1.0 / (ex[0] + ex[1] + ex[2])
    merged_t = (ex[0] * o0_ref[...] + ex[1] * o1_ref[...] + ex[2] * o2_ref[...]) * inv_den
    merged = merged_t.T.astype(BF16)
    mix = _mm(merged, wo_ref[...])
    nb = x_ref.shape[0]
    out_ref[...] = _layer_norm(ALPHA * x_ref[...] + mix[0:nb, :], g_ref[...], b_ref[...])


def _merge_step(outs, lses, x, wo, ln_g, ln_b):
    ins = (*outs, *lses, x, wo, ln_g, ln_b)
    return pl.pallas_call(
        _merge_step_body,
        out_shape=jax.ShapeDtypeStruct(x.shape, F32),
        grid=(1,),
        in_specs=[_const_spec(a.shape) for a in ins],
        out_specs=_const_spec(x.shape),
        compiler_params=_cparams(("arbitrary",)),
        name="merge_wo_ln_step",
    )(*ins)


def _merge_body(o0_ref, o1_ref, o2_ref, l0_ref, l1_ref, l2_ref, x_ref, wo_ref, e_ref,
                g_ref, b_ref, out_ref, oslab_ref, lslab_ref, merged_ref):
    o_refs = (o0_ref, o1_ref, o2_ref)
    l_refs = (l0_ref, l1_ref, l2_ref)
    for gi, (_, dil) in enumerate(ATTN_GROUPS):
        _store_classes(oslab_ref.at[gi], o_refs[gi], dil)
        _store_classes(lslab_ref.at[gi], l_refs[gi], dil)
    ls = [lslab_ref[gi, 0] for gi in range(N_GROUPS)]
    m = jnp.maximum(jnp.maximum(ls[0], ls[1]), ls[2])
    ex = [jnp.exp(l - m) for l in ls]
    inv_den = 1.0 / (ex[0] + ex[1] + ex[2])
    expand = e_ref[...]
    wfs = []
    for ex_g in ex:
        w = ex_g * inv_den
        w_hi = w.astype(BF16)
        w_lo = (w - w_hi.astype(F32)).astype(BF16)
        wfs.append(_mm(w_hi, expand) + _mm(w_lo, expand))
    for c in range(D_MODEL // LANES):
        cs = slice(c * LANES, (c + 1) * LANES)
        t = wfs[0][:, cs] * oslab_ref[0, c] + wfs[1][:, cs] * oslab_ref[1, c] + wfs[2][:, cs] * oslab_ref[2, c]
        merged_ref[:, cs] = t.astype(BF16)
    mix = _mm(merged_ref[...], wo_ref[...])
    out_ref[0] = _layer_norm(ALPHA * x_ref[0] + mix, g_ref[...], b_ref[...])


def _merge_wo_ln(outs, lses, x3, wo, ln_g, ln_b, tm):
    batch, seq, _ = x3.shape
    head = jnp.arange(LANES)[:, None]
    lane_head = jnp.arange(D_MODEL)[None, :] // HEAD_DIM
    expand = (head == lane_head).astype(BF16)
    act = pl.BlockSpec((1, tm, D_MODEL), lambda b, i: (b, i, 0))

    def classes(dil, width):
        return pl.BlockSpec((1, dil, tm // dil, width), lambda b, i: (b, 0, i, 0))

    pairs = D_MODEL // LANES
    return pl.pallas_call(
        _merge_body,
        out_shape=jax.ShapeDtypeStruct((batch, seq, D_MODEL), F32),
        grid=(batch, seq // tm),
        in_specs=[classes(dil, D_MODEL) for _, dil in ATTN_GROUPS]
        + [classes(dil, LANES) for _, dil in ATTN_GROUPS]
        + [act, _const_spec((D_MODEL, D_MODEL)), _const_spec((LANES, D_MODEL)),
           _const_spec((1, D_MODEL)), _const_spec((1, D_MODEL))],
        out_specs=act,
        scratch_shapes=[pltpu.VMEM((N_GROUPS, pairs, tm, LANES), F32),
                        pltpu.VMEM((N_GROUPS, 1, tm, LANES), F32),
                        pltpu.VMEM((tm, D_MODEL), BF16)],
        compiler_params=_cparams(("arbitrary", "arbitrary")),
        name="merge_wo_ln",
    )(*outs, *lses, x3, wo, expand, ln_g, ln_b)


def _load_halo(ext_ref, tm, tiles_per_seq):
    first = (pl.program_id(0) % tiles_per_seq) == 0

    @pl.when(first)
    def _():
        ext_ref[0:HALO, :] = jnp.zeros((HALO, ext_ref.shape[1]), F32)

    @pl.when(jnp.logical_not(first))
    def _():
        ext_ref[0:HALO, :] = ext_ref[tm:tm + HALO, :]


def _ffn_body(x_ref, wu_ref, k_ref, wd_ref, g_ref, b_ref, out_ref, st_ref,
              xb_ref, a_ref, acc_ref, *, tm, tiles_per_seq):
    _load_halo(a_ref, tm, tiles_per_seq)
    xb_ref[...] = x_ref[...].astype(BF16)
    for c in range(D_FF // FF_CHUNK):
        cs = slice(c * FF_CHUNK, (c + 1) * FF_CHUNK)
        gs = slice(D_FF + c * FF_CHUNK, D_FF + (c + 1) * FF_CHUNK)
        a = _mm(xb_ref[...], wu_ref[:, cs])
        gate = _mm(xb_ref[...], wu_ref[:, gs])
        a_ref[HALO:HALO + tm, cs] = a
        a1 = a_ref[HALO - 1:HALO - 1 + tm, cs]
        a2 = a_ref[HALO - 2:HALO - 2 + tm, cs]
        ac = a2 * k_ref[0:1, cs] + a1 * k_ref[1:2, cs] + a * k_ref[2:3, cs]
        hid = (_gelu(ac) * gate).astype(BF16)
        d = _mm(hid, wd_ref[cs, :])
        if c == 0:
            acc_ref[...] = d
        else:
            acc_ref[...] += d
    out_ref[...] = _layer_norm(ALPHA * x_ref[...] + acc_ref[...], g_ref[...], b_ref[...])
    st_ref[0] = a_ref[HALO + tm - (CONV_W - 1):HALO + tm, :]


def _ffn_prompt(x, wu, k, wd, layer, ln_g, ln_b, batch, seq, tm):
    m = x.shape[0]
    tiles_per_seq = seq // tm
    act = pl.BlockSpec((tm, D_MODEL), lambda i: (i, 0))
    return pl.pallas_call(
        functools.partial(_ffn_body, tm=tm, tiles_per_seq=tiles_per_seq),
        out_shape=(jax.ShapeDtypeStruct((m, D_MODEL), F32),
                   jax.ShapeDtypeStruct((batch, CONV_W - 1, D_FF), F32)),
        grid=(m // tm,),
        in_specs=[act, _layer_spec((D_MODEL, 2 * D_FF), layer), _const_spec((CONV_W, D_FF)),
                  _layer_spec((D_FF, D_MODEL), layer), _const_spec((1, D_MODEL)), _const_spec((1, D_MODEL))],
        out_specs=(act, pl.BlockSpec((1, CONV_W - 1, D_FF), lambda i: (i // tiles_per_seq, 0, 0))),
        scratch_shapes=[pltpu.VMEM((tm, D_MODEL), BF16),
                        pltpu.VMEM((HALO + tm, D_FF), F32),
                        pltpu.VMEM((tm, D_MODEL), F32)],
        compiler_params=_cparams(("arbitrary",)),
        name="conv_ffn",
    )(x, wu, k, wd, ln_g, ln_b)


def _ffn_step_body(x_ref, st_ref, wu_ref, k_ref, wd_ref, g_ref, b_ref, out_ref, nst_ref):
    xb = x_ref[...].astype(BF16)
    a = _mm(xb, wu_ref[:, 0:D_FF])
    gate = _mm(xb, wu_ref[:, D_FF:2 * D_FF])
    a2 = st_ref[:, 0:D_FF]
    a1 = st_ref[:, D_FF:2 * D_FF]
    ac = a2 * k_ref[0:1, :] + a1 * k_ref[1:2, :] + a * k_ref[2:3, :]
    hid = (_gelu(ac) * gate).astype(BF16)
    f = _mm(hid, wd_ref[...])
    out_ref[...] = _layer_norm(ALPHA * x_ref[...] + f, g_ref[...], b_ref[...])
    nst_ref[:, 0:D_FF] = a1
    nst_ref[:, D_FF:2 * D_FF] = a


def _ffn_step(x, st, wu, k, wd, layer, ln_g, ln_b):
    nb = x.shape[0]
    st2 = st.reshape(nb, (CONV_W - 1) * D_FF)
    out, nst = pl.pallas_call(
        _ffn_step_body,
        out_shape=(jax.ShapeDtypeStruct(x.shape, F32), jax.ShapeDtypeStruct(st2.shape, F32)),
        grid=(1,),
        in_specs=[_const_spec(x.shape), _const_spec(st2.shape), _layer_spec(wu.shape[1:], layer),
                  _const_spec(k.shape), _layer_spec(wd.shape[1:], layer),
                  _const_spec(ln_g.shape), _const_spec(ln_b.shape)],
        out_specs=(_const_spec(x.shape), _const_spec(st2.shape)),
        compiler_params=_cparams(("arbitrary",)),
        name="conv_ffn_step",
    )(x, st2, wu, k, wd, ln_g, ln_b)
    return out, nst.reshape(st.shape)


def _pool_body(x_ref, win_ref, wg_ref, sc_ref, wout_ref, g_ref, b_ref, out_ref, st_ref,
               u_ref, *, tm, tiles_per_seq):
    _load_halo(u_ref, tm, tiles_per_seq)
    xb = x_ref[...].astype(BF16)
    u_ref[HALO:HALO + tm, :] = _mm(xb, win_ref[...])
    pos = (pl.program_id(0) % tiles_per_seq) * tm + lax.broadcasted_iota(jnp.int32, (tm, 1), 0)
    zs = []
    for gi, w in enumerate(POOL_WINDOWS):
        cs = slice(gi * POOL_GROUP, (gi + 1) * POOL_GROUP)
        u = u_ref[HALO:HALO + tm, cs]
        win_sum = u
        for j in range(1, w):
            win_sum = win_sum + u_ref[HALO - j:HALO - j + tm, cs]
        cnt = jnp.minimum(w, pos + 1).astype(F32)
        pooled = win_sum * (1.0 / cnt) - u
        z = _mm(pooled.astype(BF16), wg_ref[gi])
        zs.append(z * sc_ref[:, cs])
    zc = jnp.concatenate(zs, axis=1).astype(BF16)
    mix = _mm(zc, wout_ref[...])
    out_ref[...] = _layer_norm(ALPHA * x_ref[...] + mix, g_ref[...], b_ref[...])
    st_ref[0] = u_ref[HALO + tm - POOL_PAST:HALO + tm, :]


def _pool_prompt(x, win, wg, sc, wout, ln_g, ln_b, batch, seq, tm):
    m = x.shape[0]
    tiles_per_seq = seq // tm
    act = pl.BlockSpec((tm, D_MODEL), lambda i: (i, 0))
    return pl.pallas_call(
        functools.partial(_pool_body, tm=tm, tiles_per_seq=tiles_per_seq),
        out_shape=(jax.ShapeDtypeStruct((m, D_MODEL), F32),
                   jax.ShapeDtypeStruct((batch, POOL_PAST, D_MODEL), F32)),
        grid=(m // tm,),
        in_specs=[act, _const_spec(win.shape), _const_spec(wg.shape), _const_spec(sc.shape),
                  _const_spec(wout.shape), _const_spec((1, D_MODEL)), _const_spec((1, D_MODEL))],
        out_specs=(act, pl.BlockSpec((1, POOL_PAST, D_MODEL), lambda i: (i // tiles_per_seq, 0, 0))),
        scratch_shapes=[pltpu.VMEM((HALO + tm, D_MODEL), F32)],
        compiler_params=_cparams(("arbitrary",)),
        name="pool_mixer",
    )(x, win, wg, sc, wout, ln_g, ln_b)


def _pool_step_body(x_ref, st_ref, win_ref, wg_ref, sc_ref, wout_ref, g_ref, b_ref,
                    out_ref, nst_ref):
    xb = x_ref[...].astype(BF16)
    u_all = _mm(xb, win_ref[...])
    zs = []
    for gi, w in enumerate(POOL_WINDOWS):
        u = u_all[:, gi * POOL_GROUP:(gi + 1) * POOL_GROUP]
        win_sum = u
        for j in range(1, w):
            base = (POOL_PAST - j) * D_MODEL + gi * POOL_GROUP
            win_sum = win_sum + st_ref[:, base:base + POOL_GROUP]
        cnt = float(min(w, PAST_LEN + 1))
        pooled = win_sum * (1.0 / cnt) - u
        z = _mm(pooled.astype(BF16), wg_ref[gi])
        zs.append(z * sc_ref[:, gi * POOL_GROUP:(gi + 1) * POOL_GROUP])
    zc = jnp.concatenate(zs, axis=1).astype(BF16)
    mix = _mm(zc, wout_ref[...])
    out_ref[...] = _layer_norm(ALPHA * x_ref[...] + mix, g_ref[...], b_ref[...])
    keep = (POOL_PAST - 1) * D_MODEL
    nst_ref[:, 0:keep] = st_ref[:, D_MODEL:POOL_PAST * D_MODEL]
    nst_ref[:, keep:keep + D_MODEL] = u_all


def _pool_step(x, st, win, wg, sc, wout, ln_g, ln_b):
    nb = x.shape[0]
    st2 = st.reshape(nb, POOL_PAST * D_MODEL)
    ins = (x, st2, win, wg, sc, wout, ln_g, ln_b)
    out, nst = pl.pallas_call(
        _pool_step_body,
        out_shape=(jax.ShapeDtypeStruct(x.shape, F32), jax.ShapeDtypeStruct(st2.shape, F32)),
        grid=(1,),
        in_specs=[_const_spec(a.shape) for a in ins],
        out_specs=(_const_spec(x.shape), _const_spec(st2.shape)),
        compiler_params=_cparams(("arbitrary",)),
        name="pool_mixer_step",
    )(*ins)
    return out, nst.reshape(st.shape)


def _sconv_body(x_ref, win_ref, k_ref, wout_ref, g_ref, b_ref, out_ref, st_ref,
                z_ref, *, tm, tiles_per_seq):
    _load_halo(z_ref, tm, tiles_per_seq)
    xb = x_ref[...].astype(BF16)
    gb = _mm(xb, win_ref[:, 0:D_MODEL])
    gc = _mm(xb, win_ref[:, D_MODEL:2 * D_MODEL])
    hh = _mm(xb, win_ref[:, 2 * D_MODEL:3 * D_MODEL])
    z = gc * hh
    z_ref[HALO:HALO + tm, :] = z
    z1 = z_ref[HALO - 1:HALO - 1 + tm, :]
    z2 = z_ref[HALO - 2:HALO - 2 + tm, :]
    zc = z2 * k_ref[0:1, :] + z1 * k_ref[1:2, :] + z * k_ref[2:3, :]
    mix = _mm((gb * zc).astype(BF16), wout_ref[...])
    out_ref[...] = _layer_norm(ALPHA * x_ref[...] + mix, g_ref[...], b_ref[...])
    st_ref[0] = z_ref[HALO + tm - (CONV_W - 1):HALO + tm, :]


def _sconv_prompt(x, win, k, wout, ln_g, ln_b, batch, seq, tm):
    m = x.shape[0]
    tiles_per_seq = seq // tm
    act = pl.BlockSpec((tm, D_MODEL), lambda i: (i, 0))
    return pl.pallas_call(
        functools.partial(_sconv_body, tm=tm, tiles_per_seq=tiles_per_seq),
        out_shape=(jax.ShapeDtypeStruct((m, D_MODEL), F32),
                   jax.ShapeDtypeStruct((batch, CONV_W - 1, D_MODEL), F32)),
        grid=(m // tm,),
        in_specs=[act, _const_spec(win.shape), _const_spec(k.shape), _const_spec(wout.shape),
                  _const_spec((1, D_MODEL)), _const_spec((1, D_MODEL))],
        out_specs=(act, pl.BlockSpec((1, CONV_W - 1, D_MODEL), lambda i: (i // tiles_per_seq, 0, 0))),
        scratch_shapes=[pltpu.VMEM((HALO + tm, D_MODEL), F32)],
        compiler_params=_cparams(("arbitrary",)),
        name="sconv_mixer",
    )(x, win, k, wout, ln_g, ln_b)


def _sconv_step_body(x_ref, st_ref, win_ref, k_ref, wout_ref, g_ref, b_ref, out_ref, nst_ref):
    xb = x_ref[...].astype(BF16)
    gb = _mm(xb, win_ref[:, 0:D_MODEL])
    gc = _mm(xb, win_ref[:, D_MODEL:2 * D_MODEL])
    hh = _mm(xb, win_ref[:, 2 * D_MODEL:3 * D_MODEL])
    z = gc * hh
    z2 = st_ref[:, 0:D_MODEL]
    z1 = st_ref[:, D_MODEL:2 * D_MODEL]
    zc = z2 * k_ref[0:1, :] + z1 * k_ref[1:2, :] + z * k_ref[2:3, :]
    mix = _mm((gb * zc).astype(BF16), wout_ref[...])
    out_ref[...] = _layer_norm(ALPHA * x_ref[...] + mix, g_ref[...], b_ref[...])
    nst_ref[:, 0:D_MODEL] = z1
    nst_ref[:, D_MODEL:2 * D_MODEL] = z


def _sconv_step(x, st, win, k, wout, ln_g, ln_b):
    nb = x.shape[0]
    st2 = st.reshape(nb, (CONV_W - 1) * D_MODEL)
    ins = (x, st2, win, k, wout, ln_g, ln_b)
    out, nst = pl.pallas_call(
        _sconv_step_body,
        out_shape=(jax.ShapeDtypeStruct(x.shape, F32), jax.ShapeDtypeStruct(st2.shape, F32)),
        grid=(1,),
        in_specs=[_const_spec(a.shape) for a in ins],
        out_specs=(_const_spec(x.shape), _const_spec(st2.shape)),
        compiler_params=_cparams(("arbitrary",)),
        name="sconv_mixer_step",
    )(*ins)
    return out, nst.reshape(st.shape)


def kernel(x_prompt, x_sample, cache_kv_w128, cache_kv_w512, cache_kv_w2048, state_pool, state_sconv, state_ffn_conv, attn_w_qkv, attn_w_o, pool_w_in, pool_w_grp, pool_scale, pool_w_out, sconv_w_in, sconv_k, sconv_w_out, ffn_w_up, ffn_k, ffn_w_down, ln_g, ln_b):
    batch, seq, _ = x_prompt.shape
    nb = x_sample.shape[0]
    caches = (cache_kv_w128, cache_kv_w512, cache_kv_w2048)

    w_qkv = attn_w_qkv.astype(BF16)
    w_o = attn_w_o.astype(BF16)
    p_in = pool_w_in.astype(BF16)
    p_grp = pool_w_grp.astype(BF16)
    p_out = pool_w_out.astype(BF16)
    s_in = sconv_w_in.astype(BF16)
    s_out = sconv_w_out.astype(BF16)
    f_up = ffn_w_up.astype(BF16)
    f_down = ffn_w_down.astype(BF16)

    tables = [_rope_tables(jnp.arange(seq).reshape(seq // dil, dil).T.reshape(seq)) for _, dil in ATTN_GROUPS]
    cos_p = [t[0] for t in tables]
    sin_p = [t[1] for t in tables]
    cos_s, sin_s = _rope_tables(jnp.full((nb,), PAST_LEN))

    xp = x_prompt.reshape(batch * seq, D_MODEL)
    xs = x_sample.reshape(nb, D_MODEL)

    caches_t = [c.transpose(0, 1, 3, 4, 5, 2) for c in caches]
    n_attn = caches[0].shape[0]
    kv_p = [None for _ in ATTN_GROUPS]
    kv_s = [None for _ in ATTN_GROUPS]
    pool_p, pool_s, sconv_p, sconv_s, ffn_p, ffn_s = [], [], [], [], [], []
    ia = ib = ic = 0
    tm_qkv = 1024
    tm = 512
    for i in range(DEPTH):
        kind = i % N_MIXERS
        g1 = ln_g[i, 0].reshape(1, D_MODEL)
        b1 = ln_b[i, 0].reshape(1, D_MODEL)
        g2 = ln_g[i, 1].reshape(1, D_MODEL)
        b2 = ln_b[i, 1].reshape(1, D_MODEL)
        if kind == 0:
            x3 = xp.reshape(batch, seq, D_MODEL)
            outs_p, lses_p, outs_s, lses_s = [], [], [], []
            for g, (win, dil) in enumerate(ATTN_GROUPS):
                n = seq // dil
                xg = x3 if dil == 1 else _to_classes(x3, dil, tm_qkv)
                qkv_g = _qkv_rope(xg.reshape(batch * seq, D_MODEL), w_qkv, ia, g,
                                  cos_p[g], sin_p[g], tm_qkv, seq // tm_qkv, BF16)
                o, l = _attn_prompt(qkv_g.reshape(batch * dil, n, 3 * D_MODEL), g)
                outs_p.append(o.reshape(batch, dil, n, D_MODEL))
                lses_p.append(l.reshape(batch, dil, n, LANES))
                kv_p[g] = _kv_tail(x3, w_qkv, cos_p[0], sin_p[0], kv_p[g], ia, n_attn, g, min(win, seq))
                qkv_s = _qkv_rope(xs, w_qkv, ia, g, cos_s, sin_s, nb, 1, F32)
                kv_s[g], o, l = _cache_step(qkv_s.T, caches_t[g], kv_s[g], ia, g, dil)
                outs_s.append(o)
                lses_s.append(l)
            xp = _merge_wo_ln(outs_p, lses_p, x3, w_o[ia], g1, b1, tm).reshape(batch * seq, D_MODEL)
            xs = _merge_step(outs_s, lses_s, xs, w_o[ia], g1, b1)
            ia += 1
        elif kind == 1:
            sc = pool_scale[ib].reshape(1, D_MODEL)
            xp, st = _pool_prompt(xp, p_in[ib], p_grp[ib], sc, p_out[ib], g1, b1, batch, seq, tm)
            pool_p.append(st)
            xs, st = _pool_step(xs, state_pool[ib], p_in[ib], p_grp[ib], sc, p_out[ib], g1, b1)
            pool_s.append(st)
            ib += 1
        else:
            xp, st = _sconv_prompt(xp, s_in[ic], sconv_k[ic], s_out[ic], g1, b1, batch, seq, tm)
            sconv_p.append(st)
            xs, st = _sconv_step(xs, state_sconv[ic], s_in[ic], sconv_k[ic], s_out[ic], g1, b1)
            sconv_s.append(st)
            ic += 1
        xp, st = _ffn_prompt(xp, f_up, ffn_k[i], f_down, i, g2, b2, batch, seq, tm)
        ffn_p.append(st)
        xs, st = _ffn_step(xs, state_ffn_conv[i], f_up, ffn_k[i], f_down, i, g2, b2)
        ffn_s.append(st)

    y_prompt = xp.reshape(batch, seq, D_MODEL)
    y_sample = xs.reshape(nb, 1, D_MODEL)

    def time_major(t, rows):
        t = t.reshape(n_attn, rows, 2, N_HEADS, HEAD_DIM, t.shape[-1])
        return t.transpose(0, 1, 5, 2, 3, 4)

    return (y_prompt, y_sample,
            time_major(kv_p[0], batch), time_major(kv_s[0], nb),
            time_major(kv_p[1], batch), time_major(kv_s[1], nb),
            time_major(kv_p[2], batch), time_major(kv_s[2], nb),
            jnp.stack(pool_p), jnp.stack(pool_s),
            jnp.stack(sconv_p), jnp.stack(sconv_s),
            jnp.stack(ffn_p), jnp.stack(ffn_s))
```

```python
import functools

import jax
import jax.numpy as jnp
from jax import lax
from jax.experimental import pallas as pl
from jax.experimental.pallas import tpu as pltpu

F32 = jnp.float32
BF16 = jnp.bfloat16

D_MODEL = 1024
DEPTH = 4
PAST_LEN = 8192
N_MIXERS = 3
ATTN_GROUPS = ((128, 1), (512, 4), (2048, 16))
N_GROUPS = len(ATTN_GROUPS)
HEAD_DIM = 64
N_HEADS = D_MODEL // HEAD_DIM
ATTN_SCALE = HEAD_DIM ** -0.5
ROPE_THETA = 10000.0
POOL_WINDOWS = (2, 4, 8, 16)
POOL_GROUP = D_MODEL // len(POOL_WINDOWS)
POOL_PAST = max(POOL_WINDOWS) - 1
CONV_W = 3
D_FF = ((8 * D_MODEL // 3 + 127) // 128) * 128
ALPHA = (2.0 * DEPTH) ** 0.25
LN_EPS = 1e-5
QKV_COLS = N_GROUPS * 3 * D_MODEL

LANES = 128
HALO = 16
ATTN_BLOCK = 128
FF_CHUNK = 256
CACHE_BLOCK_BYTES = 8 * 1024 * 1024
KV_TAIL_TILE = 512
QKV_ROW_CHUNK = 128
ATTN_STEP_ROWS = 512
VMEM_LIMIT = 56 * 1024 * 1024
NEG = -1e30
LOG2E = 1.4426950408889634
LN2 = 0.6931471805599453


def _cparams(sem):
    return pltpu.CompilerParams(dimension_semantics=sem, vmem_limit_bytes=VMEM_LIMIT)


def _layer_norm(z, g, b):
    mu = jnp.mean(z, axis=-1, keepdims=True)
    zc = z - mu
    var = jnp.mean(zc * zc, axis=-1, keepdims=True)
    return zc * lax.rsqrt(var + LN_EPS) * g + b


def _mm(a, w):
    return jnp.dot(a, w, preferred_element_type=F32)


def _gelu(x):
    return 0.5 * x * (1.0 + lax.erf(x * (2.0 ** -0.5)))


def _const_spec(shape):
    nd = len(shape)
    return pl.BlockSpec(shape, lambda *_: (0,) * nd)


def _layer_spec(shape, layer, resident=False):
    nd = len(shape)
    mode = pl.Buffered(1) if resident else None
    return pl.BlockSpec((None,) + tuple(shape), lambda *_: (layer,) + (0,) * nd, pipeline_mode=mode)


def _qkv_body(x_ref, w_ref, cos_ref, sin_ref, o_ref, xb_ref, *, chunk, q_scale):
    xb_ref[...] = x_ref[...].astype(BF16)
    tm = x_ref.shape[0]
    lane = lax.broadcasted_iota(jnp.int32, (chunk, LANES), 1)
    first_half = (lane & (HEAD_DIM // 2)) == 0
    for unit in range(3):
        for rc in range(tm // chunk):
            rs = slice(rc * chunk, (rc + 1) * chunk)
            y = _mm(xb_ref[rs, :], w_ref[:, unit * D_MODEL:(unit + 1) * D_MODEL])
            if unit == 2:
                o_ref[rs, unit * D_MODEL:(unit + 1) * D_MODEL] = y.astype(o_ref.dtype)
                continue
            cos = cos_ref[rs, :]
            sin = sin_ref[rs, :]
            if unit == 0 and q_scale != 1.0:
                cos = cos * q_scale
                sin = sin * q_scale
            for c in range(D_MODEL // LANES):
                yc = y[:, c * LANES:(c + 1) * LANES]
                partner = jnp.where(first_half,
                                    pltpu.roll(yc, LANES - HEAD_DIM // 2, 1),
                                    pltpu.roll(yc, HEAD_DIM // 2, 1))
                col = unit * D_MODEL + c * LANES
                o_ref[rs, col:col + LANES] = (yc * cos + partner * sin).astype(o_ref.dtype)


def _qkv_rope(x, w, layer, g, cos, sin, tm, pos_tiles, out_dtype, q_scale=1.0):
    m = x.shape[0]
    return pl.pallas_call(
        functools.partial(_qkv_body, chunk=min(tm, QKV_ROW_CHUNK), q_scale=q_scale),
        out_shape=jax.ShapeDtypeStruct((m, 3 * D_MODEL), out_dtype),
        grid=(m // tm,),
        in_specs=[
            pl.BlockSpec((tm, D_MODEL), lambda i: (i, 0)),
            pl.BlockSpec((None, D_MODEL, 3 * D_MODEL), lambda i: (layer, 0, g)),
            pl.BlockSpec((tm, LANES), lambda i: (i % pos_tiles, 0)),
            pl.BlockSpec((tm, LANES), lambda i: (i % pos_tiles, 0)),
        ],
        out_specs=pl.BlockSpec((tm, 3 * D_MODEL), lambda i: (i, 0)),
        scratch_shapes=[pltpu.VMEM((tm, D_MODEL), BF16)],
        compiler_params=_cparams(("arbitrary",)),
        name=f"qkv_rope_g{g}",
    )(x, w, cos, sin)


def _store_classes(slab_ref, src_ref, dil):
    n = src_ref.shape[2]
    for r in range(dil):
        for c in range(src_ref.shape[3] // LANES):
            slab_ref[c, pl.ds(r, n, stride=dil), :] = src_ref[0, r, :, c * LANES:(c + 1) * LANES]


def _to_classes_body(x_ref, o_ref, slab_ref, *, dil):
    n = o_ref.shape[2]
    for c in range(D_MODEL // LANES):
        slab_ref[c] = x_ref[0, :, c * LANES:(c + 1) * LANES]
    for r in range(dil):
        for c in range(D_MODEL // LANES):
            o_ref[0, r, :, c * LANES:(c + 1) * LANES] = slab_ref[c, pl.ds(r, n, stride=dil), :]


def _to_classes(x3, dil, tm):
    batch, seq, _ = x3.shape
    return pl.pallas_call(
        functools.partial(_to_classes_body, dil=dil),
        out_shape=jax.ShapeDtypeStruct((batch, dil, seq // dil, D_MODEL), F32),
        grid=(batch, seq // tm),
        in_specs=[pl.BlockSpec((1, tm, D_MODEL), lambda b, i: (b, i, 0))],
        out_specs=pl.BlockSpec((1, dil, tm // dil, D_MODEL), lambda b, i: (b, 0, i, 0)),
        scratch_shapes=[pltpu.VMEM((D_MODEL // LANES, tm, LANES), F32)],
        compiler_params=_cparams(("arbitrary", "arbitrary")),
        name=f"to_classes_d{dil}",
    )(x3)


def _rope_tables(pos):
    half = HEAD_DIM // 2
    inv = ROPE_THETA ** (-jnp.arange(half, dtype=F32) / half)
    ang = pos.astype(F32)[:, None] * inv[None, :]
    cos = jnp.cos(ang)
    sin = jnp.sin(ang)
    reps = LANES // HEAD_DIM
    cos_t = jnp.tile(jnp.concatenate([cos, cos], axis=1), (1, reps))
    sin_t = jnp.tile(jnp.concatenate([-sin, sin], axis=1), (1, reps))
    return cos_t, sin_t


def _band_mask(has_prev):
    tq = ATTN_BLOCK
    row = lax.broadcasted_iota(jnp.int32, (tq, 2 * tq), 0)
    col = lax.broadcasted_iota(jnp.int32, (tq, 2 * tq), 1)
    delta = row + tq - col
    return (delta >= 0) & (delta <= tq) & ((col >= tq) | has_prev)


def _attn_pair(qp, k2, v2, valid, low):
    outs, lses = [], []
    for half in range(2):
        sel = low if half == 0 else jnp.logical_not(low)
        qm = jnp.where(sel, qp, 0.0)
        s = lax.dot_general(qm, k2, (((1,), (1,)), ((), ())), preferred_element_type=F32)
        s = jnp.where(valid, s, NEG)
        m = jnp.max(s, axis=1, keepdims=True)
        p = jnp.exp2(s - m)
        l = jnp.sum(p, axis=1, keepdims=True)
        pv = _mm(p.astype(BF16), v2)
        outs.append(pv * (1.0 / l))
        lses.append((m + jnp.log2(l)) * LN2)
    return jnp.where(low, outs[0], outs[1]), lses[0], lses[1]


def _attn_body(q_ref, kp_ref, kc_ref, vp_ref, vc_ref, o_ref, lse_ref, k_ref, v_ref):
    rows = q_ref.shape[1]
    not_first = pl.program_id(1) > 0
    k_ref[0:ATTN_BLOCK, :] = kp_ref[0]
    k_ref[ATTN_BLOCK:ATTN_BLOCK + rows, :] = kc_ref[0]
    v_ref[0:ATTN_BLOCK, :] = vp_ref[0]
    v_ref[ATTN_BLOCK:ATTN_BLOCK + rows, :] = vc_ref[0]
    lane = lax.broadcasted_iota(jnp.int32, (ATTN_BLOCK, LANES), 1)
    low = lane < HEAD_DIM
    for sb in range(rows // ATTN_BLOCK):
        valid = _band_mask(True if sb > 0 else not_first)
        qs = slice(sb * ATTN_BLOCK, (sb + 1) * ATTN_BLOCK)
        ks = slice(sb * ATTN_BLOCK, (sb + 2) * ATTN_BLOCK)
        lse_full = jnp.zeros((ATTN_BLOCK, LANES), F32)
        for hp in range(D_MODEL // LANES):
            cs = slice(hp * LANES, (hp + 1) * LANES)
            o, lse0, lse1 = _attn_pair(q_ref[0, qs, cs], k_ref[ks, cs], v_ref[ks, cs], valid, low)
            o_ref[0, qs, cs] = o
            lse_full = jnp.where(lane == 2 * hp, lse0, lse_full)
            lse_full = jnp.where(lane == 2 * hp + 1, lse1, lse_full)
        lse_ref[0, qs, :] = lse_full


def _attn_prompt(qkv3, g):
    nseq, n, _ = qkv3.shape
    rows = min(n, ATTN_STEP_ROWS)
    per_step = rows // ATTN_BLOCK
    blk = (1, rows, D_MODEL)

    def cur(unit):
        return pl.BlockSpec(blk, lambda b, i: (b, i, unit))

    def prev(unit):
        return pl.BlockSpec((1, ATTN_BLOCK, D_MODEL),
                            lambda b, i: (b, jnp.maximum(i * per_step - 1, 0), unit))

    return pl.pallas_call(
        _attn_body,
        out_shape=(jax.ShapeDtypeStruct((nseq, n, D_MODEL), F32),
                   jax.ShapeDtypeStruct((nseq, n, LANES), F32)),
        grid=(nseq, n // rows),
        in_specs=[cur(0), prev(1), cur(1), prev(2), cur(2)],
        out_specs=(pl.BlockSpec(blk, lambda b, i: (b, i, 0)),
                   pl.BlockSpec((1, rows, LANES), lambda b, i: (b, i, 0))),
        scratch_shapes=[pltpu.VMEM((ATTN_BLOCK + rows, D_MODEL), BF16),
                        pltpu.VMEM((ATTN_BLOCK + rows, D_MODEL), BF16)],
        compiler_params=_cparams(("arbitrary", "arbitrary")),
        name=f"attn_prompt_g{g}",
    )(qkv3, qkv3, qkv3, qkv3, qkv3)


def _cache_step_body(*refs, heads, win, dil, aliased):
    q_ref, knr_ref, vnr_ref, knt_ref, vnt_ref, c_ref = refs[:6]
    nc_ref, o_ref, lse_ref = refs[-3:]
    b = pl.program_id(0)
    hc = pl.program_id(1)
    rows = heads * HEAD_DIM
    sub = 16
    rowh = lax.broadcasted_iota(jnp.int32, (sub, rows), 0)
    own = rowh == lax.broadcasted_iota(jnp.int32, (sub, rows), 1) // HEAD_DIM
    q = q_ref[0] * ATTN_SCALE
    qbd = jnp.where(own, jnp.broadcast_to(q, (sub, rows)), 0.0).astype(BF16)
    k = c_ref[0]
    v = c_ref[1]
    t = lax.broadcasted_iota(jnp.int32, (sub, win), 1)
    s = jnp.where((t & (dil - 1)) == 0, _mm(qbd, k.astype(BF16)), NEG)
    knr = knr_ref[0].astype(BF16).astype(F32)
    vnr = vnr_ref[0].astype(BF16).astype(F32)
    s_new = jnp.sum(qbd.astype(F32) * knr, axis=1, keepdims=True)
    m = jnp.maximum(jnp.max(s, axis=1, keepdims=True), s_new)
    p = jnp.exp(s - m)
    p_new = jnp.exp(s_new - m)
    l = jnp.sum(p, axis=1, keepdims=True) + p_new
    pv = lax.dot_general(p.astype(BF16), v.astype(BF16), (((1,), (1,)), ((), ())),
                         preferred_element_type=F32)
    o16 = (pv + p_new.astype(BF16).astype(F32) * vnr) * (1.0 / l)
    o_ref[0] = jnp.sum(jnp.where(own, o16, 0.0), axis=0, keepdims=True)
    lse = m + jnp.log(l)
    r2 = lax.broadcasted_iota(jnp.int32, (sub, LANES), 0)
    l2 = lax.broadcasted_iota(jnp.int32, (sub, LANES), 1)
    lse_row = jnp.sum(jnp.where((r2 < heads) & (r2 + hc * heads == l2), lse, 0.0), axis=0, keepdims=True)

    @pl.when(hc == 0)
    def _():
        lse_ref[0] = lse_row

    @pl.when(hc > 0)
    def _():
        lse_ref[0] = lse_ref[0] + lse_row

    pick = lax.broadcasted_iota(jnp.int32, knt_ref.shape, 1) == b
    kn_col = jnp.sum(jnp.where(pick, knt_ref[...], 0.0), axis=1, keepdims=True)
    vn_col = jnp.sum(jnp.where(pick, vnt_ref[...], 0.0), axis=1, keepdims=True)
    last = lax.broadcasted_iota(jnp.int32, (rows, win), 1) == win - 1
    nc_ref[0] = jnp.where(last, kn_col, pltpu.roll(k, win - 1, 1))
    nc_ref[1] = jnp.where(last, vn_col, pltpu.roll(v, win - 1, 1))


def _cache_step(qkv_s, cache_t, prev_out, layer, g, dil):
    n_layers, nb, _, _, win = cache_t.shape
    heads = max(1, min(N_HEADS, CACHE_BLOCK_BYTES // (2 * HEAD_DIM * win * 4)))
    rows = heads * HEAD_DIM
    per_unit = D_MODEL // rows
    aliased = prev_out is not None
    qkv_s3 = qkv_s.reshape(nb, 1, 3 * D_MODEL)
    qkv_t = qkv_s.T

    def row(u):
        return pl.BlockSpec((1, 1, rows), lambda b, hc: (b, 0, u * per_unit + hc))

    def col(u):
        return pl.BlockSpec((rows, nb), lambda b, hc: (u * per_unit + hc, 0))

    cblk = (None, None, 2, rows, win)
    in_specs = [row(0), row(1), row(2), col(1), col(2),
                pl.BlockSpec(cblk, lambda b, hc: (layer, b, 0, hc, 0))]
    args = [qkv_s3, qkv_s3, qkv_s3, qkv_t, qkv_t, cache_t]
    if aliased:
        in_specs.append(pl.BlockSpec(memory_space=pl.ANY))
        args.append(prev_out)
    return pl.pallas_call(
        functools.partial(_cache_step_body, heads=heads, win=win, dil=dil, aliased=aliased),
        out_shape=(jax.ShapeDtypeStruct(cache_t.shape, F32),
                   jax.ShapeDtypeStruct((nb, 1, D_MODEL), F32),
                   jax.ShapeDtypeStruct((nb, 1, LANES), F32)),
        grid=(nb, N_HEADS // heads),
        in_specs=in_specs,
        out_specs=(pl.BlockSpec(cblk, lambda b, hc: (layer, b, 0, hc, 0)),
                   pl.BlockSpec((1, 1, rows), lambda b, hc: (b, 0, hc)),
                   pl.BlockSpec((1, 1, LANES), lambda b, hc: (b, 0, 0))),
        input_output_aliases={6: 0} if aliased else {},
        compiler_params=_cparams(("arbitrary", "arbitrary")),
        name=f"cache_step_g{g}",
    )(*args)


def _kv_tail_body(*refs):
    x_ref, w_ref, cos_ref, sin_ref = refs[:4]
    o_ref = refs[-1]
    is_v = pl.program_id(2) == 1
    cos = jnp.where(is_v, 1.0, cos_ref[...])
    sin = jnp.where(is_v, 0.0, sin_ref[...])
    lane = lax.broadcasted_iota(jnp.int32, cos.shape, 1)
    first_half = (lane & (HEAD_DIM // 2)) == 0
    y = _mm(x_ref[0].astype(BF16), w_ref[...])
    for c in range(D_MODEL // LANES):
        cs = slice(c * LANES, (c + 1) * LANES)
        yc = y[:, cs]
        partner = jnp.where(first_half,
                            pltpu.roll(yc, LANES - HEAD_DIM // 2, 1),
                            pltpu.roll(yc, HEAD_DIM // 2, 1))
        o_ref[cs, :] = (yc * cos + partner * sin).T


def _kv_tail(x3, w, cos, sin, prev_out, layer, n_layers, g, win):
    batch, seq, _ = x3.shape
    tt = min(win, KV_TAIL_TILE)
    first = (seq - win) // tt
    aliased = prev_out is not None
    table = pl.BlockSpec((tt, LANES), lambda b, i, kv: (first + i, 0))
    in_specs = [pl.BlockSpec((1, tt, D_MODEL), lambda b, i, kv: (b, first + i, 0)),
                pl.BlockSpec((None, D_MODEL, D_MODEL), lambda b, i, kv: (layer, 0, 3 * g + 1 + kv)),
                table, table]
    args = [x3, w, cos, sin]
    if aliased:
        in_specs.append(pl.BlockSpec(memory_space=pl.ANY))
        args.append(prev_out)
    return pl.pallas_call(
        _kv_tail_body,
        out_shape=jax.ShapeDtypeStruct((n_layers, batch, 2, D_MODEL, win), F32),
        grid=(batch, win // tt, 2),
        in_specs=in_specs,
        out_specs=pl.BlockSpec((None, None, None, D_MODEL, tt), lambda b, i, kv: (layer, b, kv, 0, i)),
        input_output_aliases={4: 0} if aliased else {},
        compiler_params=_cparams(("arbitrary", "arbitrary", "arbitrary")),
        name=f"kv_tail_g{g}",
    )(*args)


def _group_weights(ls, expand):
    m = jnp.maximum(jnp.maximum(ls[0], ls[1]), ls[2])
    ex = [jnp.exp(l - m) for l in ls]
    inv_den = 1.0 / (ex[0] + ex[1] + ex[2])
    wfs = []
    for ex_g in ex:
        w = ex_g * inv_den
        w_hi = w.astype(BF16)
        w_lo = (w - w_hi.astype(F32)).astype(BF16)
        wfs.append(_mm(w_hi, expand) + _mm(w_lo, expand))
    return wfs


def _merge_step_body(o0_ref, o1_ref, o2_ref, l0_ref, l1_ref, l2_ref, x_ref, wo_ref, e_ref,
                     g_ref, b_ref, out_ref):
    wfs = _group_weights([l0_ref[...], l1_ref[...], l2_ref[...]], e_ref[...])
    merged = wfs[0] * o0_ref[...] + wfs[1] * o1_ref[...] + wfs[2] * o2_ref[...]
    mix = _mm(merged.astype(BF16), wo_ref[...])
    out_ref[...] = _layer_norm(ALPHA * x_ref[...] + mix, g_ref[...], b_ref[...])


def _head_expand():
    head = jnp.arange(LANES)[:, None]
    lane_head = jnp.arange(D_MODEL)[None, :] // HEAD_DIM
    return (head == lane_head).astype(BF16)


def _merge_step(outs, lses, x, wo, ln_g, ln_b):
    ins = (*outs, *lses, x, wo, _head_expand(), ln_g, ln_b)
    return pl.pallas_call(
        _merge_step_body,
        out_shape=jax.ShapeDtypeStruct(x.shape, F32),
        grid=(1,),
        in_specs=[_const_spec(a.shape) for a in ins],
        out_specs=_const_spec(x.shape),
        compiler_params=_cparams(("arbitrary",)),
        name="merge_wo_ln_step",
    )(*ins)


def _merge_body(o0_ref, o1_ref, o2_ref, l0_ref, l1_ref, l2_ref, x_ref, wo_ref, e_ref,
                g_ref, b_ref, out_ref, oslab_ref, lslab_ref, merged_ref):
    o_refs = (o0_ref, o1_ref, o2_ref)
    l_refs = (l0_ref, l1_ref, l2_ref)
    for gi, (_, dil) in enumerate(ATTN_GROUPS):
        _store_classes(oslab_ref.at[gi], o_refs[gi], dil)
        _store_classes(lslab_ref.at[gi], l_refs[gi], dil)
    wfs = _group_weights([lslab_ref[gi, 0] for gi in range(N_GROUPS)], e_ref[...])
    for c in range(D_MODEL // LANES):
        cs = slice(c * LANES, (c + 1) * LANES)
        t = wfs[0][:, cs] * oslab_ref[0, c] + wfs[1][:, cs] * oslab_ref[1, c] + wfs[2][:, cs] * oslab_ref[2, c]
        merged_ref[:, cs] = t.astype(BF16)
    mix = _mm(merged_ref[...], wo_ref[...])
    out_ref[0] = _layer_norm(ALPHA * x_ref[0] + mix, g_ref[...], b_ref[...])


def _merge_wo_ln(outs, lses, x3, wo, ln_g, ln_b, tm):
    batch, seq, _ = x3.shape
    expand = _head_expand()
    act = pl.BlockSpec((1, tm, D_MODEL), lambda b, i: (b, i, 0))

    def classes(dil, width):
        return pl.BlockSpec((1, dil, tm // dil, width), lambda b, i: (b, 0, i, 0))

    pairs = D_MODEL // LANES
    return pl.pallas_call(
        _merge_body,
        out_shape=jax.ShapeDtypeStruct((batch, seq, D_MODEL), F32),
        grid=(batch, seq // tm),
        in_specs=[classes(dil, D_MODEL) for _, dil in ATTN_GROUPS]
        + [classes(dil, LANES) for _, dil in ATTN_GROUPS]
        + [act, _const_spec((D_MODEL, D_MODEL)), _const_spec((LANES, D_MODEL)),
           _const_spec((1, D_MODEL)), _const_spec((1, D_MODEL))],
        out_specs=act,
        scratch_shapes=[pltpu.VMEM((N_GROUPS, pairs, tm, LANES), F32),
                        pltpu.VMEM((N_GROUPS, 1, tm, LANES), F32),
                        pltpu.VMEM((tm, D_MODEL), BF16)],
        compiler_params=_cparams(("arbitrary", "arbitrary")),
        name="merge_wo_ln",
    )(*outs, *lses, x3, wo, expand, ln_g, ln_b)


def _load_halo(ext_ref, tm, tiles_per_seq):
    first = (pl.program_id(0) % tiles_per_seq) == 0

    @pl.when(first)
    def _():
        ext_ref[0:HALO, :] = jnp.zeros((HALO, ext_ref.shape[1]), F32)

    @pl.when(jnp.logical_not(first))
    def _():
        ext_ref[0:HALO, :] = ext_ref[tm:tm + HALO, :]


def _ffn_body(x_ref, wu_ref, k_ref, wd_ref, g_ref, b_ref, out_ref, st_ref,
              xb_ref, a_ref, carry_ref, acc_ref, *, tm, tiles_per_seq):
    @pl.when((pl.program_id(0) % tiles_per_seq) == 0)
    def _():
        carry_ref[...] = jnp.zeros(carry_ref.shape, F32)

    xb_ref[...] = x_ref[...].astype(BF16)

    def up(c):
        a = _mm(xb_ref[...], wu_ref[:, c * FF_CHUNK:(c + 1) * FF_CHUNK])
        gate = _mm(xb_ref[...], wu_ref[:, D_FF + c * FF_CHUNK:D_FF + (c + 1) * FF_CHUNK])
        return a, gate

    n_chunks = D_FF // FF_CHUNK
    nxt = up(0)
    for c in range(n_chunks):
        cs = slice(c * FF_CHUNK, (c + 1) * FF_CHUNK)
        buf = a_ref.at[c % 2]
        a, gate = nxt
        if c + 1 < n_chunks:
            nxt = up(c + 1)
        buf[0:HALO, :] = carry_ref[:, cs]
        buf[HALO:HALO + tm, :] = a
        a1 = buf[HALO - 1:HALO - 1 + tm, :]
        a2 = buf[HALO - 2:HALO - 2 + tm, :]
        carry_ref[:, cs] = buf[tm:tm + HALO, :]
        ac = a2 * k_ref[0:1, cs] + a1 * k_ref[1:2, cs] + a * k_ref[2:3, cs]
        hid = (_gelu(ac) * gate).astype(BF16)
        d = _mm(hid, wd_ref[cs, :])
        if c == 0:
            acc_ref[...] = d
        else:
            acc_ref[...] += d
    out_ref[...] = _layer_norm(ALPHA * x_ref[...] + acc_ref[...], g_ref[...], b_ref[...])
    st_ref[0] = carry_ref[HALO - (CONV_W - 1):HALO, :]


def _ffn_prompt(x, wu, k, wd, layer, ln_g, ln_b, batch, seq, tm):
    m = x.shape[0]
    tiles_per_seq = seq // tm
    act = pl.BlockSpec((tm, D_MODEL), lambda i: (i, 0))
    return pl.pallas_call(
        functools.partial(_ffn_body, tm=tm, tiles_per_seq=tiles_per_seq),
        out_shape=(jax.ShapeDtypeStruct((m, D_MODEL), F32),
                   jax.ShapeDtypeStruct((batch, CONV_W - 1, D_FF), F32)),
        grid=(m // tm,),
        in_specs=[act, _layer_spec((D_MODEL, 2 * D_FF), layer, resident=True), _const_spec((CONV_W, D_FF)),
                  _layer_spec((D_FF, D_MODEL), layer, resident=True),
                  _const_spec((1, D_MODEL)), _const_spec((1, D_MODEL))],
        out_specs=(act, pl.BlockSpec((1, CONV_W - 1, D_FF), lambda i: (i // tiles_per_seq, 0, 0))),
        scratch_shapes=[pltpu.VMEM((tm, D_MODEL), BF16),
                        pltpu.VMEM((2, HALO + tm, FF_CHUNK), F32),
                        pltpu.VMEM((HALO, D_FF), F32),
                        pltpu.VMEM((tm, D_MODEL), F32)],
        compiler_params=_cparams(("arbitrary",)),
        name="conv_ffn",
    )(x, wu, k, wd, ln_g, ln_b)


def _ffn_step_body(x_ref, st_ref, wu_ref, k_ref, wd_ref, g_ref, b_ref, out_ref, nst_ref):
    xb = x_ref[...].astype(BF16)
    a = _mm(xb, wu_ref[:, 0:D_FF])
    gate = _mm(xb, wu_ref[:, D_FF:2 * D_FF])
    a2 = st_ref[:, 0:D_FF]
    a1 = st_ref[:, D_FF:2 * D_FF]
    ac = a2 * k_ref[0:1, :] + a1 * k_ref[1:2, :] + a * k_ref[2:3, :]
    hid = (_gelu(ac) * gate).astype(BF16)
    f = _mm(hid, wd_ref[...])
    out_ref[...] = _layer_norm(ALPHA * x_ref[...] + f, g_ref[...], b_ref[...])
    nst_ref[:, 0:D_FF] = a1
    nst_ref[:, D_FF:2 * D_FF] = a


def _ffn_step(x, st, wu, k, wd, layer, ln_g, ln_b):
    nb = x.shape[0]
    st2 = st.reshape(nb, (CONV_W - 1) * D_FF)
    out, nst = pl.pallas_call(
        _ffn_step_body,
        out_shape=(jax.ShapeDtypeStruct(x.shape, F32), jax.ShapeDtypeStruct(st2.shape, F32)),
        grid=(1,),
        in_specs=[_const_spec(x.shape), _const_spec(st2.shape), _layer_spec(wu.shape[1:], layer),
                  _const_spec(k.shape), _layer_spec(wd.shape[1:], layer),
                  _const_spec(ln_g.shape), _const_spec(ln_b.shape)],
        out_specs=(_const_spec(x.shape), _const_spec(st2.shape)),
        compiler_params=_cparams(("arbitrary",)),
        name="conv_ffn_step",
    )(x, st2, wu, k, wd, ln_g, ln_b)
    return out, nst.reshape(st.shape)


def _pool_body(x_ref, win_ref, wg_ref, sc_ref, wout_ref, g_ref, b_ref, out_ref, st_ref,
               u_ref, *, tm, tiles_per_seq):
    _load_halo(u_ref, tm, tiles_per_seq)
    xb = x_ref[...].astype(BF16)
    u_ref[HALO:HALO + tm, :] = _mm(xb, win_ref[...])
    pos = (pl.program_id(0) % tiles_per_seq) * tm + lax.broadcasted_iota(jnp.int32, (tm, 1), 0)
    zs = []
    for gi, w in enumerate(POOL_WINDOWS):
        cs = slice(gi * POOL_GROUP, (gi + 1) * POOL_GROUP)
        u = u_ref[HALO:HALO + tm, cs]
        win_sum = u
        for j in range(1, w):
            win_sum = win_sum + u_ref[HALO - j:HALO - j + tm, cs]
        cnt = jnp.minimum(w, pos + 1).astype(F32)
        pooled = win_sum * (1.0 / cnt) - u
        z = _mm(pooled.astype(BF16), wg_ref[gi])
        zs.append(z * sc_ref[:, cs])
    zc = jnp.concatenate(zs, axis=1).astype(BF16)
    mix = _mm(zc, wout_ref[...])
    out_ref[...] = _layer_norm(ALPHA * x_ref[...] + mix, g_ref[...], b_ref[...])
    st_ref[0] = u_ref[HALO + tm - POOL_PAST:HALO + tm, :]


def _pool_prompt(x, win, wg, sc, wout, ln_g, ln_b, batch, seq, tm):
    m = x.shape[0]
    tiles_per_seq = seq // tm
    act = pl.BlockSpec((tm, D_MODEL), lambda i: (i, 0))
    return pl.pallas_call(
        functools.partial(_pool_body, tm=tm, tiles_per_seq=tiles_per_seq),
        out_shape=(jax.ShapeDtypeStruct((m, D_MODEL), F32),
                   jax.ShapeDtypeStruct((batch, POOL_PAST, D_MODEL), F32)),
        grid=(m // tm,),
        in_specs=[act, _const_spec(win.shape), _const_spec(wg.shape), _const_spec(sc.shape),
                  _const_spec(wout.shape), _const_spec((1, D_MODEL)), _const_spec((1, D_MODEL))],
        out_specs=(act, pl.BlockSpec((1, POOL_PAST, D_MODEL), lambda i: (i // tiles_per_seq, 0, 0))),
        scratch_shapes=[pltpu.VMEM((HALO + tm, D_MODEL), F32)],
        compiler_params=_cparams(("arbitrary",)),
        name="pool_mixer",
    )(x, win, wg, sc, wout, ln_g, ln_b)


def _pool_step_body(x_ref, st_ref, win_ref, wg_ref, sc_ref, wout_ref, g_ref, b_ref,
                    out_ref, nst_ref):
    xb = x_ref[...].astype(BF16)
    u_all = _mm(xb, win_ref[...])
    zs = []
    for gi, w in enumerate(POOL_WINDOWS):
        u = u_all[:, gi * POOL_GROUP:(gi + 1) * POOL_GROUP]
        win_sum = u
        for j in range(1, w):
            base = (POOL_PAST - j) * D_MODEL + gi * POOL_GROUP
            win_sum = win_sum + st_ref[:, base:base + POOL_GROUP]
        cnt = float(min(w, PAST_LEN + 1))
        pooled = win_sum * (1.0 / cnt) - u
        z = _mm(pooled.astype(BF16), wg_ref[gi])
        zs.append(z * sc_ref[:, gi * POOL_GROUP:(gi + 1) * POOL_GROUP])
    zc = jnp.concatenate(zs, axis=1).astype(BF16)
    mix = _mm(zc, wout_ref[...])
    out_ref[...] = _layer_norm(ALPHA * x_ref[...] + mix, g_ref[...], b_ref[...])
    keep = (POOL_PAST - 1) * D_MODEL
    nst_ref[:, 0:keep] = st_ref[:, D_MODEL:POOL_PAST * D_MODEL]
    nst_ref[:, keep:keep + D_MODEL] = u_all


def _pool_step(x, st, win, wg, sc, wout, ln_g, ln_b):
    nb = x.shape[0]
    st2 = st.reshape(nb, POOL_PAST * D_MODEL)
    ins = (x, st2, win, wg, sc, wout, ln_g, ln_b)
    out, nst = pl.pallas_call(
        _pool_step_body,
        out_shape=(jax.ShapeDtypeStruct(x.shape, F32), jax.ShapeDtypeStruct(st2.shape, F32)),
        grid=(1,),
        in_specs=[_const_spec(a.shape) for a in ins],
        out_specs=(_const_spec(x.shape), _const_spec(st2.shape)),
        compiler_params=_cparams(("arbitrary",)),
        name="pool_mixer_step",
    )(*ins)
    return out, nst.reshape(st.shape)


def _sconv_body(x_ref, win_ref, k_ref, wout_ref, g_ref, b_ref, out_ref, st_ref,
                z_ref, *, tm, tiles_per_seq):
    _load_halo(z_ref, tm, tiles_per_seq)
    xb = x_ref[...].astype(BF16)
    gb = _mm(xb, win_ref[:, 0:D_MODEL])
    gc = _mm(xb, win_ref[:, D_MODEL:2 * D_MODEL])
    hh = _mm(xb, win_ref[:, 2 * D_MODEL:3 * D_MODEL])
    z = gc * hh
    z_ref[HALO:HALO + tm, :] = z
    z1 = z_ref[HALO - 1:HALO - 1 + tm, :]
    z2 = z_ref[HALO - 2:HALO - 2 + tm, :]
    zc = z2 * k_ref[0:1, :] + z1 * k_ref[1:2, :] + z * k_ref[2:3, :]
    mix = _mm((gb * zc).astype(BF16), wout_ref[...])
    out_ref[...] = _layer_norm(ALPHA * x_ref[...] + mix, g_ref[...], b_ref[...])
    st_ref[0] = z_ref[HALO + tm - (CONV_W - 1):HALO + tm, :]


def _sconv_prompt(x, win, k, wout, ln_g, ln_b, batch, seq, tm):
    m = x.shape[0]
    tiles_per_seq = seq // tm
    act = pl.BlockSpec((tm, D_MODEL), lambda i: (i, 0))
    return pl.pallas_call(
        functools.partial(_sconv_body, tm=tm, tiles_per_seq=tiles_per_seq),
        out_shape=(jax.ShapeDtypeStruct((m, D_MODEL), F32),
                   jax.ShapeDtypeStruct((batch, CONV_W - 1, D_MODEL), F32)),
        grid=(m // tm,),
        in_specs=[act, _const_spec(win.shape), _const_spec(k.shape), _const_spec(wout.shape),
                  _const_spec((1, D_MODEL)), _const_spec((1, D_MODEL))],
        out_specs=(act, pl.BlockSpec((1, CONV_W - 1, D_MODEL), lambda i: (i // tiles_per_seq, 0, 0))),
        scratch_shapes=[pltpu.VMEM((HALO + tm, D_MODEL), F32)],
        compiler_params=_cparams(("arbitrary",)),
        name="sconv_mixer",
    )(x, win, k, wout, ln_g, ln_b)


def _sconv_step_body(x_ref, st_ref, win_ref, k_ref, wout_ref, g_ref, b_ref, out_ref, nst_ref):
    xb = x_ref[...].astype(BF16)
    gb = _mm(xb, win_ref[:, 0:D_MODEL])
    gc = _mm(xb, win_ref[:, D_MODEL:2 * D_MODEL])
    hh = _mm(xb, win_ref[:, 2 * D_MODEL:3 * D_MODEL])
    z = gc * hh
    z2 = st_ref[:, 0:D_MODEL]
    z1 = st_ref[:, D_MODEL:2 * D_MODEL]
    zc = z2 * k_ref[0:1, :] + z1 * k_ref[1:2, :] + z * k_ref[2:3, :]
    mix = _mm((gb * zc).astype(BF16), wout_ref[...])
    out_ref[...] = _layer_norm(ALPHA * x_ref[...] + mix, g_ref[...], b_ref[...])
    nst_ref[:, 0:D_MODEL] = z1
    nst_ref[:, D_MODEL:2 * D_MODEL] = z


def _sconv_step(x, st, win, k, wout, ln_g, ln_b):
    nb = x.shape[0]
    st2 = st.reshape(nb, (CONV_W - 1) * D_MODEL)
    ins = (x, st2, win, k, wout, ln_g, ln_b)
    out, nst = pl.pallas_call(
        _sconv_step_body,
        out_shape=(jax.ShapeDtypeStruct(x.shape, F32), jax.ShapeDtypeStruct(st2.shape, F32)),
        grid=(1,),
        in_specs=[_const_spec(a.shape) for a in ins],
        out_specs=(_const_spec(x.shape), _const_spec(st2.shape)),
        compiler_params=_cparams(("arbitrary",)),
        name="sconv_mixer_step",
    )(*ins)
    return out, nst.reshape(st.shape)


def kernel(x_prompt, x_sample, cache_kv_w128, cache_kv_w512, cache_kv_w2048, state_pool, state_sconv, state_ffn_conv, attn_w_qkv, attn_w_o, pool_w_in, pool_w_grp, pool_scale, pool_w_out, sconv_w_in, sconv_k, sconv_w_out, ffn_w_up, ffn_k, ffn_w_down, ln_g, ln_b):
    batch, seq, _ = x_prompt.shape
    nb = x_sample.shape[0]
    caches = (cache_kv_w128, cache_kv_w512, cache_kv_w2048)

    w_qkv = attn_w_qkv.astype(BF16)
    w_o = attn_w_o.astype(BF16)
    p_in = pool_w_in.astype(BF16)
    p_grp = pool_w_grp.astype(BF16)
    p_out = pool_w_out.astype(BF16)
    s_in = sconv_w_in.astype(BF16)
    s_out = sconv_w_out.astype(BF16)
    f_up = ffn_w_up.astype(BF16)
    f_down = ffn_w_down.astype(BF16)

    tables = [_rope_tables(jnp.arange(seq).reshape(seq // dil, dil).T.reshape(seq)) for _, dil in ATTN_GROUPS]
    cos_p = [t[0] for t in tables]
    sin_p = [t[1] for t in tables]
    cos_s, sin_s = _rope_tables(jnp.full((nb,), PAST_LEN))

    xp = x_prompt.reshape(batch * seq, D_MODEL)
    xs = x_sample.reshape(nb, D_MODEL)

    caches_t = [c.transpose(0, 1, 3, 4, 5, 2).reshape(c.shape[0], c.shape[1], 2, D_MODEL, c.shape[2])
                for c in caches]
    n_attn = caches[0].shape[0]
    kv_p = [None for _ in ATTN_GROUPS]
    kv_s = [None for _ in ATTN_GROUPS]
    pool_p, pool_s, sconv_p, sconv_s, ffn_p, ffn_s = [], [], [], [], [], []
    ia = ib = ic = 0
    tm_qkv = 1024
    tm_ffn = 1024
    tm = 512
    for i in range(DEPTH):
        kind = i % N_MIXERS
        g1 = ln_g[i, 0].reshape(1, D_MODEL)
        b1 = ln_b[i, 0].reshape(1, D_MODEL)
        g2 = ln_g[i, 1].reshape(1, D_MODEL)
        b2 = ln_b[i, 1].reshape(1, D_MODEL)
        if kind == 0:
            x3 = xp.reshape(batch, seq, D_MODEL)
            outs_p, lses_p, outs_s, lses_s = [], [], [], []
            for g, (win, dil) in enumerate(ATTN_GROUPS):
                n = seq // dil
                xg = x3 if dil == 1 else _to_classes(x3, dil, tm_qkv)
                qkv_g = _qkv_rope(xg.reshape(batch * seq, D_MODEL), w_qkv, ia, g,
                                  cos_p[g], sin_p[g], tm_qkv, seq // tm_qkv, BF16,
                                  q_scale=ATTN_SCALE * LOG2E)
                o, l = _attn_prompt(qkv_g.reshape(batch * dil, n, 3 * D_MODEL), g)
                outs_p.append(o.reshape(batch, dil, n, D_MODEL))
                lses_p.append(l.reshape(batch, dil, n, LANES))
                kv_p[g] = _kv_tail(x3, w_qkv, cos_p[0], sin_p[0], kv_p[g], ia, n_attn, g, min(win, seq))
                qkv_s = _qkv_rope(xs, w_qkv, ia, g, cos_s, sin_s, nb, 1, F32)
                kv_s[g], o, l = _cache_step(qkv_s, caches_t[g], kv_s[g], ia, g, dil)
                outs_s.append(o.reshape(nb, D_MODEL))
                lses_s.append(l.reshape(nb, LANES))
            xp = _merge_wo_ln(outs_p, lses_p, x3, w_o[ia], g1, b1, tm).reshape(batch * seq, D_MODEL)
            xs = _merge_step(outs_s, lses_s, xs, w_o[ia], g1, b1)
            ia += 1
        elif kind == 1:
            sc = pool_scale[ib].reshape(1, D_MODEL)
            xp, st = _pool_prompt(xp, p_in[ib], p_grp[ib], sc, p_out[ib], g1, b1, batch, seq, tm)
            pool_p.append(st)
            xs, st = _pool_step(xs, state_pool[ib], p_in[ib], p_grp[ib], sc, p_out[ib], g1, b1)
            pool_s.append(st)
            ib += 1
        else:
            xp, st = _sconv_prompt(xp, s_in[ic], sconv_k[ic], s_out[ic], g1, b1, batch, seq, tm)
            sconv_p.append(st)
            xs, st = _sconv_step(xs, state_sconv[ic], s_in[ic], sconv_k[ic], s_out[ic], g1, b1)
            sconv_s.append(st)
            ic += 1
        xp, st = _ffn_prompt(xp, f_up, ffn_k[i], f_down, i, g2, b2, batch, seq, tm_ffn)
        ffn_p.append(st)
        xs, st = _ffn_step(xs, state_ffn_conv[i], f_up, ffn_k[i], f_down, i, g2, b2)
        ffn_s.append(st)

    y_prompt = xp.reshape(batch, seq, D_MODEL)
    y_sample = xs.reshape(nb, 1, D_MODEL)

    def time_major(t, rows):
        t = t.reshape(n_attn, rows, 2, N_HEADS, HEAD_DIM, t.shape[-1])
        return t.transpose(0, 1, 5, 2, 3, 4)

    return (y_prompt, y_sample,
            time_major(kv_p[0], batch), time_major(kv_s[0], nb),
            time_major(kv_p[1], batch), time_major(kv_s[1], nb),
            time_major(kv_p[2], batch), time_major(kv_s[2], nb),
            jnp.stack(pool_p), jnp.stack(pool_s),
            jnp.stack(sconv_p), jnp.stack(sconv_s),
            jnp.stack(ffn_p), jnp.stack(ffn_s))
```

```python
import functools

import jax
import jax.numpy as jnp
from jax import lax
from jax.experimental import pallas as pl
from jax.experimental.pallas import tpu as pltpu

F32 = jnp.float32
BF16 = jnp.bfloat16

D_MODEL = 1024
DEPTH = 4
PAST_LEN = 8192
N_MIXERS = 3
ATTN_GROUPS = ((128, 1), (512, 4), (2048, 16))
N_GROUPS = len(ATTN_GROUPS)
HEAD_DIM = 64
N_HEADS = D_MODEL // HEAD_DIM
ATTN_SCALE = HEAD_DIM ** -0.5
ROPE_THETA = 10000.0
POOL_WINDOWS = (2, 4, 8, 16)
POOL_GROUP = D_MODEL // len(POOL_WINDOWS)
POOL_PAST = max(POOL_WINDOWS) - 1
CONV_W = 3
D_FF = ((8 * D_MODEL // 3 + 127) // 128) * 128
ALPHA = (2.0 * DEPTH) ** 0.25
LN_EPS = 1e-5
QKV_COLS = N_GROUPS * 3 * D_MODEL

LANES = 128
HALO = 16
ATTN_BLOCK = 128
FF_CHUNK = 256
CACHE_BLOCK_BYTES = 8 * 1024 * 1024
KV_TAIL_TILE = 512
QKV_ROW_CHUNK = 128
ATTN_STEP_ROWS = 512
VMEM_LIMIT = 56 * 1024 * 1024
NEG = -1e30
LOG2E = 1.4426950408889634
LN2 = 0.6931471805599453


def _cparams(sem):
    return pltpu.CompilerParams(dimension_semantics=sem, vmem_limit_bytes=VMEM_LIMIT)


def _layer_norm(z, g, b):
    mu = jnp.mean(z, axis=-1, keepdims=True)
    zc = z - mu
    var = jnp.mean(zc * zc, axis=-1, keepdims=True)
    return zc * lax.rsqrt(var + LN_EPS) * g + b


def _mm(a, w):
    return jnp.dot(a, w, preferred_element_type=F32)


def _gelu(x):
    return 0.5 * x * (1.0 + lax.erf(x * (2.0 ** -0.5)))


def _const_spec(shape):
    nd = len(shape)
    return pl.BlockSpec(shape, lambda *_: (0,) * nd)


def _layer_spec(shape, layer, resident=False):
    nd = len(shape)
    mode = pl.Buffered(1) if resident else None
    return pl.BlockSpec((None,) + tuple(shape), lambda *_: (layer,) + (0,) * nd, pipeline_mode=mode)


def _qkv_body(x_ref, w_ref, cos_ref, sin_ref, o_ref, xb_ref, slab_ref, *, chunk, q_scale, dil):
    tm = x_ref.shape[1]
    n = tm // dil
    if dil == 1:
        xb_ref[...] = x_ref[0].astype(BF16)
    else:
        for c in range(D_MODEL // LANES):
            slab_ref[c] = x_ref[0, :, c * LANES:(c + 1) * LANES]
        for r in range(dil):
            for c in range(D_MODEL // LANES):
                xb_ref[r * n:(r + 1) * n, c * LANES:(c + 1) * LANES] = (
                    slab_ref[c, pl.ds(r, n, stride=dil), :].astype(BF16))
    piece = min(chunk, n)
    lane = lax.broadcasted_iota(jnp.int32, (chunk, LANES), 1)
    first_half = (lane & (HEAD_DIM // 2)) == 0

    def store(rc, col, val):
        for p in range(chunk // piece):
            off = rc * chunk + p * piece
            o_ref[0, off // n, off % n:off % n + piece, col:col + val.shape[1]] = (
                val[p * piece:(p + 1) * piece].astype(o_ref.dtype))

    for unit in range(3):
        for rc in range(tm // chunk):
            rs = slice(rc * chunk, (rc + 1) * chunk)
            y = _mm(xb_ref[rs, :], w_ref[:, unit * D_MODEL:(unit + 1) * D_MODEL])
            if unit == 2:
                store(rc, unit * D_MODEL, y)
                continue
            cos = cos_ref[rs, :]
            sin = sin_ref[rs, :]
            if unit == 0 and q_scale != 1.0:
                cos = cos * q_scale
                sin = sin * q_scale
            for c in range(D_MODEL // LANES):
                yc = y[:, c * LANES:(c + 1) * LANES]
                partner = jnp.where(first_half,
                                    pltpu.roll(yc, LANES - HEAD_DIM // 2, 1),
                                    pltpu.roll(yc, HEAD_DIM // 2, 1))
                store(rc, unit * D_MODEL + c * LANES, yc * cos + partner * sin)


def _qkv_rope(x3, w, layer, g, dil, cos, sin, tm, out_dtype, q_scale=1.0):
    nseq, seq, _ = x3.shape
    table = pl.BlockSpec((tm, LANES), lambda b, i: (i, 0))
    return pl.pallas_call(
        functools.partial(_qkv_body, chunk=min(tm, QKV_ROW_CHUNK), q_scale=q_scale, dil=dil),
        out_shape=jax.ShapeDtypeStruct((nseq, dil, seq // dil, 3 * D_MODEL), out_dtype),
        grid=(nseq, seq // tm),
        in_specs=[
            pl.BlockSpec((1, tm, D_MODEL), lambda b, i: (b, i, 0)),
            pl.BlockSpec((None, D_MODEL, 3 * D_MODEL), lambda b, i: (layer, 0, g)),
            table, table,
        ],
        out_specs=pl.BlockSpec((1, dil, tm // dil, 3 * D_MODEL), lambda b, i: (b, 0, i, 0)),
        scratch_shapes=[pltpu.VMEM((tm, D_MODEL), BF16),
                        pltpu.VMEM((D_MODEL // LANES, tm if dil > 1 else 8, LANES), F32)],
        compiler_params=_cparams(("arbitrary", "arbitrary")),
        name=f"qkv_rope_g{g}",
    )(x3, w, cos, sin)


def _store_classes(slab_ref, src_ref, dil):
    n = src_ref.shape[2]
    for r in range(dil):
        for c in range(src_ref.shape[3] // LANES):
            slab_ref[c, pl.ds(r, n, stride=dil), :] = src_ref[0, r, :, c * LANES:(c + 1) * LANES]


def _rope_tables(pos):
    half = HEAD_DIM // 2
    inv = ROPE_THETA ** (-jnp.arange(half, dtype=F32) / half)
    ang = pos.astype(F32)[:, None] * inv[None, :]
    cos = jnp.cos(ang)
    sin = jnp.sin(ang)
    reps = LANES // HEAD_DIM
    cos_t = jnp.tile(jnp.concatenate([cos, cos], axis=1), (1, reps))
    sin_t = jnp.tile(jnp.concatenate([-sin, sin], axis=1), (1, reps))
    return cos_t, sin_t


def _band_mask(has_prev):
    tq = ATTN_BLOCK
    row = lax.broadcasted_iota(jnp.int32, (tq, 2 * tq), 0)
    col = lax.broadcasted_iota(jnp.int32, (tq, 2 * tq), 1)
    delta = row + tq - col
    return (delta >= 0) & (delta <= tq) & ((col >= tq) | has_prev)


def _attn_pair(qp, k2, v2, valid, low):
    outs, lses = [], []
    for half in range(2):
        sel = low if half == 0 else jnp.logical_not(low)
        qm = jnp.where(sel, qp, 0.0)
        s = lax.dot_general(qm, k2, (((1,), (1,)), ((), ())), preferred_element_type=F32)
        s = jnp.where(valid, s, NEG)
        m = jnp.max(s, axis=1, keepdims=True)
        p = jnp.exp2(s - m)
        l = jnp.sum(p, axis=1, keepdims=True)
        pv = _mm(p.astype(BF16), v2)
        outs.append(pv * (1.0 / l))
        lses.append((m + jnp.log2(l)) * LN2)
    return jnp.where(low, outs[0], outs[1]), lses[0], lses[1]


def _attn_body(q_ref, kp_ref, kc_ref, vp_ref, vc_ref, o_ref, lse_ref, k_ref, v_ref):
    rows = q_ref.shape[1]
    not_first = pl.program_id(1) > 0
    k_ref[0:ATTN_BLOCK, :] = kp_ref[0]
    k_ref[ATTN_BLOCK:ATTN_BLOCK + rows, :] = kc_ref[0]
    v_ref[0:ATTN_BLOCK, :] = vp_ref[0]
    v_ref[ATTN_BLOCK:ATTN_BLOCK + rows, :] = vc_ref[0]
    lane = lax.broadcasted_iota(jnp.int32, (ATTN_BLOCK, LANES), 1)
    low = lane < HEAD_DIM
    for sb in range(rows // ATTN_BLOCK):
        valid = _band_mask(True if sb > 0 else not_first)
        qs = slice(sb * ATTN_BLOCK, (sb + 1) * ATTN_BLOCK)
        ks = slice(sb * ATTN_BLOCK, (sb + 2) * ATTN_BLOCK)
        lse_full = jnp.zeros((ATTN_BLOCK, LANES), F32)
        for hp in range(D_MODEL // LANES):
            cs = slice(hp * LANES, (hp + 1) * LANES)
            o, lse0, lse1 = _attn_pair(q_ref[0, qs, cs], k_ref[ks, cs], v_ref[ks, cs], valid, low)
            o_ref[0, qs, cs] = o
            lse_full = jnp.where(lane == 2 * hp, lse0, lse_full)
            lse_full = jnp.where(lane == 2 * hp + 1, lse1, lse_full)
        lse_ref[0, qs, :] = lse_full


def _attn_prompt(qkv3, g):
    nseq, n, _ = qkv3.shape
    rows = min(n, ATTN_STEP_ROWS)
    per_step = rows // ATTN_BLOCK
    blk = (1, rows, D_MODEL)

    def cur(unit):
        return pl.BlockSpec(blk, lambda b, i: (b, i, unit))

    def prev(unit):
        return pl.BlockSpec((1, ATTN_BLOCK, D_MODEL),
                            lambda b, i: (b, jnp.maximum(i * per_step - 1, 0), unit))

    return pl.pallas_call(
        _attn_body,
        out_shape=(jax.ShapeDtypeStruct((nseq, n, D_MODEL), F32),
                   jax.ShapeDtypeStruct((nseq, n, LANES), F32)),
        grid=(nseq, n // rows),
        in_specs=[cur(0), prev(1), cur(1), prev(2), cur(2)],
        out_specs=(pl.BlockSpec(blk, lambda b, i: (b, i, 0)),
                   pl.BlockSpec((1, rows, LANES), lambda b, i: (b, i, 0))),
        scratch_shapes=[pltpu.VMEM((ATTN_BLOCK + rows, D_MODEL), BF16),
                        pltpu.VMEM((ATTN_BLOCK + rows, D_MODEL), BF16)],
        compiler_params=_cparams(("arbitrary", "arbitrary")),
        name=f"attn_prompt_g{g}",
    )(qkv3, qkv3, qkv3, qkv3, qkv3)


def _cache_step_body(*refs, heads, win, dil, aliased):
    q_ref, knr_ref, vnr_ref, knt_ref, vnt_ref, c_ref = refs[:6]
    nc_ref, o_ref, lse_ref = refs[-3:]
    b = pl.program_id(0)
    hc = pl.program_id(1)
    rows = heads * HEAD_DIM
    sub = 16
    rowh = lax.broadcasted_iota(jnp.int32, (sub, rows), 0)
    own = rowh == lax.broadcasted_iota(jnp.int32, (sub, rows), 1) // HEAD_DIM
    q = q_ref[0] * ATTN_SCALE
    qbd = jnp.where(own, jnp.broadcast_to(q, (sub, rows)), 0.0).astype(BF16)
    k = c_ref[0]
    v = c_ref[1]
    t = lax.broadcasted_iota(jnp.int32, (sub, win), 1)
    s = jnp.where((t & (dil - 1)) == 0, _mm(qbd, k.astype(BF16)), NEG)
    knr = knr_ref[0].astype(BF16).astype(F32)
    vnr = vnr_ref[0].astype(BF16).astype(F32)
    s_new = jnp.sum(qbd.astype(F32) * knr, axis=1, keepdims=True)
    m = jnp.maximum(jnp.max(s, axis=1, keepdims=True), s_new)
    p = jnp.exp(s - m)
    p_new = jnp.exp(s_new - m)
    l = jnp.sum(p, axis=1, keepdims=True) + p_new
    pv = lax.dot_general(p.astype(BF16), v.astype(BF16), (((1,), (1,)), ((), ())),
                         preferred_element_type=F32)
    o16 = (pv + p_new.astype(BF16).astype(F32) * vnr) * (1.0 / l)
    o_ref[0] = jnp.sum(jnp.where(own, o16, 0.0), axis=0, keepdims=True)
    lse = m + jnp.log(l)
    r2 = lax.broadcasted_iota(jnp.int32, (sub, LANES), 0)
    l2 = lax.broadcasted_iota(jnp.int32, (sub, LANES), 1)
    lse_row = jnp.sum(jnp.where((r2 < heads) & (r2 + hc * heads == l2), lse, 0.0), axis=0, keepdims=True)

    @pl.when(hc == 0)
    def _():
        lse_ref[0] = lse_row

    @pl.when(hc > 0)
    def _():
        lse_ref[0] = lse_ref[0] + lse_row

    pick = lax.broadcasted_iota(jnp.int32, knt_ref.shape, 1) == b
    kn_col = jnp.sum(jnp.where(pick, knt_ref[...], 0.0), axis=1, keepdims=True)
    vn_col = jnp.sum(jnp.where(pick, vnt_ref[...], 0.0), axis=1, keepdims=True)
    last = lax.broadcasted_iota(jnp.int32, (rows, win), 1) == win - 1
    nc_ref[0] = jnp.where(last, kn_col, pltpu.roll(k, win - 1, 1))
    nc_ref[1] = jnp.where(last, vn_col, pltpu.roll(v, win - 1, 1))


def _cache_step(qkv_s, cache_t, prev_out, layer, g, dil):
    n_layers, nb, _, _, win = cache_t.shape
    heads = max(1, min(N_HEADS, CACHE_BLOCK_BYTES // (2 * HEAD_DIM * win * 4)))
    rows = heads * HEAD_DIM
    per_unit = D_MODEL // rows
    aliased = prev_out is not None
    qkv_s3 = qkv_s.reshape(nb, 1, 3 * D_MODEL)
    qkv_t = qkv_s.T

    def row(u):
        return pl.BlockSpec((1, 1, rows), lambda b, hc: (b, 0, u * per_unit + hc))

    def col(u):
        return pl.BlockSpec((rows, nb), lambda b, hc: (u * per_unit + hc, 0))

    cblk = (None, None, 2, rows, win)
    in_specs = [row(0), row(1), row(2), col(1), col(2),
                pl.BlockSpec(cblk, lambda b, hc: (layer, b, 0, hc, 0))]
    args = [qkv_s3, qkv_s3, qkv_s3, qkv_t, qkv_t, cache_t]
    if aliased:
        in_specs.append(pl.BlockSpec(memory_space=pl.ANY))
        args.append(prev_out)
    return pl.pallas_call(
        functools.partial(_cache_step_body, heads=heads, win=win, dil=dil, aliased=aliased),
        out_shape=(jax.ShapeDtypeStruct(cache_t.shape, F32),
                   jax.ShapeDtypeStruct((nb, 1, D_MODEL), F32),
                   jax.ShapeDtypeStruct((nb, 1, LANES), F32)),
        grid=(nb, N_HEADS // heads),
        in_specs=in_specs,
        out_specs=(pl.BlockSpec(cblk, lambda b, hc: (layer, b, 0, hc, 0)),
                   pl.BlockSpec((1, 1, rows), lambda b, hc: (b, 0, hc)),
                   pl.BlockSpec((1, 1, LANES), lambda b, hc: (b, 0, 0))),
        input_output_aliases={6: 0} if aliased else {},
        compiler_params=_cparams(("arbitrary", "arbitrary")),
        name=f"cache_step_g{g}",
    )(*args)


def _kv_tail_body(*refs):
    x_ref, w_ref, cos_ref, sin_ref = refs[:4]
    o_ref = refs[-1]
    is_v = pl.program_id(2) == 1
    chunk = LANES
    lane = lax.broadcasted_iota(jnp.int32, (chunk, LANES), 1)
    first_half = (lane & (HEAD_DIM // 2)) == 0
    xb = x_ref[0].astype(BF16)
    for rc in range(x_ref.shape[1] // chunk):
        rs = slice(rc * chunk, (rc + 1) * chunk)
        cos = jnp.where(is_v, 1.0, cos_ref[rs, :])
        sin = jnp.where(is_v, 0.0, sin_ref[rs, :])
        y = _mm(xb[rs], w_ref[...])
        for c in range(D_MODEL // LANES):
            cs = slice(c * LANES, (c + 1) * LANES)
            yc = y[:, cs]
            partner = jnp.where(first_half,
                                pltpu.roll(yc, LANES - HEAD_DIM // 2, 1),
                                pltpu.roll(yc, HEAD_DIM // 2, 1))
            o_ref[cs, rs] = (yc * cos + partner * sin).T


def _kv_tail(x3, w, cos, sin, prev_out, layer, n_layers, g, win):
    batch, seq, _ = x3.shape
    tt = min(win, KV_TAIL_TILE)
    first = (seq - win) // tt
    aliased = prev_out is not None
    table = pl.BlockSpec((tt, LANES), lambda b, i, kv: (first + i, 0))
    in_specs = [pl.BlockSpec((1, tt, D_MODEL), lambda b, i, kv: (b, first + i, 0)),
                pl.BlockSpec((None, D_MODEL, D_MODEL), lambda b, i, kv: (layer, 0, 3 * g + 1 + kv)),
                table, table]
    args = [x3, w, cos, sin]
    if aliased:
        in_specs.append(pl.BlockSpec(memory_space=pl.ANY))
        args.append(prev_out)
    return pl.pallas_call(
        _kv_tail_body,
        out_shape=jax.ShapeDtypeStruct((n_layers, batch, 2, D_MODEL, win), F32),
        grid=(batch, win // tt, 2),
        in_specs=in_specs,
        out_specs=pl.BlockSpec((None, None, None, D_MODEL, tt), lambda b, i, kv: (layer, b, kv, 0, i)),
        input_output_aliases={4: 0} if aliased else {},
        compiler_params=_cparams(("arbitrary", "arbitrary", "arbitrary")),
        name=f"kv_tail_g{g}",
    )(*args)


def _group_weights(ls, expand):
    m = jnp.maximum(jnp.maximum(ls[0], ls[1]), ls[2])
    ex = [jnp.exp(l - m) for l in ls]
    inv_den = 1.0 / (ex[0] + ex[1] + ex[2])
    wfs = []
    for ex_g in ex:
        w = ex_g * inv_den
        w_hi = w.astype(BF16)
        w_lo = (w - w_hi.astype(F32)).astype(BF16)
        wfs.append(_mm(w_hi, expand) + _mm(w_lo, expand))
    return wfs


def _merge_step_body(o0_ref, o1_ref, o2_ref, l0_ref, l1_ref, l2_ref, x_ref, wo_ref, e_ref,
                     g_ref, b_ref, out_ref):
    wfs = _group_weights([l0_ref[...], l1_ref[...], l2_ref[...]], e_ref[...])
    merged = wfs[0] * o0_ref[...] + wfs[1] * o1_ref[...] + wfs[2] * o2_ref[...]
    mix = _mm(merged.astype(BF16), wo_ref[...])
    out_ref[...] = _layer_norm(ALPHA * x_ref[...] + mix, g_ref[...], b_ref[...])


def _head_expand():
    head = jnp.arange(LANES)[:, None]
    lane_head = jnp.arange(D_MODEL)[None, :] // HEAD_DIM
    return (head == lane_head).astype(BF16)


def _merge_step(outs, lses, x, wo, ln_g, ln_b):
    ins = (*outs, *lses, x, wo, _head_expand(), ln_g, ln_b)
    return pl.pallas_call(
        _merge_step_body,
        out_shape=jax.ShapeDtypeStruct(x.shape, F32),
        grid=(1,),
        in_specs=[_const_spec(a.shape) for a in ins],
        out_specs=_const_spec(x.shape),
        compiler_params=_cparams(("arbitrary",)),
        name="merge_wo_ln_step",
    )(*ins)


def _merge_body(o0_ref, o1_ref, o2_ref, l0_ref, l1_ref, l2_ref, x_ref, wo_ref, e_ref,
                g_ref, b_ref, out_ref, oslab_ref, lslab_ref, merged_ref):
    o_refs = (o0_ref, o1_ref, o2_ref)
    l_refs = (l0_ref, l1_ref, l2_ref)
    for gi, (_, dil) in enumerate(ATTN_GROUPS):
        _store_classes(oslab_ref.at[gi], o_refs[gi], dil)
        _store_classes(lslab_ref.at[gi], l_refs[gi], dil)
    wfs = _group_weights([lslab_ref[gi, 0] for gi in range(N_GROUPS)], e_ref[...])
    for c in range(D_MODEL // LANES):
        cs = slice(c * LANES, (c + 1) * LANES)
        t = wfs[0][:, cs] * oslab_ref[0, c] + wfs[1][:, cs] * oslab_ref[1, c] + wfs[2][:, cs] * oslab_ref[2, c]
        merged_ref[:, cs] = t.astype(BF16)
    mix = _mm(merged_ref[...], wo_ref[...])
    out_ref[0] = _layer_norm(ALPHA * x_ref[0] + mix, g_ref[...], b_ref[...])


def _merge_wo_ln(outs, lses, x3, wo, ln_g, ln_b, tm):
    batch, seq, _ = x3.shape
    expand = _head_expand()
    act = pl.BlockSpec((1, tm, D_MODEL), lambda b, i: (b, i, 0))

    def classes(dil, width):
        return pl.BlockSpec((1, dil, tm // dil, width), lambda b, i: (b, 0, i, 0))

    pairs = D_MODEL // LANES
    return pl.pallas_call(
        _merge_body,
        out_shape=jax.ShapeDtypeStruct((batch, seq, D_MODEL), F32),
        grid=(batch, seq // tm),
        in_specs=[classes(dil, D_MODEL) for _, dil in ATTN_GROUPS]
        + [classes(dil, LANES) for _, dil in ATTN_GROUPS]
        + [act, _const_spec((D_MODEL, D_MODEL)), _const_spec((LANES, D_MODEL)),
           _const_spec((1, D_MODEL)), _const_spec((1, D_MODEL))],
        out_specs=act,
        scratch_shapes=[pltpu.VMEM((N_GROUPS, pairs, tm, LANES), F32),
                        pltpu.VMEM((N_GROUPS, 1, tm, LANES), F32),
                        pltpu.VMEM((tm, D_MODEL), BF16)],
        compiler_params=_cparams(("arbitrary", "arbitrary")),
        name="merge_wo_ln",
    )(*outs, *lses, x3, wo, expand, ln_g, ln_b)


def _load_halo(ext_ref, tm, tiles_per_seq):
    first = (pl.program_id(0) % tiles_per_seq) == 0

    @pl.when(first)
    def _():
        ext_ref[0:HALO, :] = jnp.zeros((HALO, ext_ref.shape[1]), F32)

    @pl.when(jnp.logical_not(first))
    def _():
        ext_ref[0:HALO, :] = ext_ref[tm:tm + HALO, :]


def _ffn_body(x_ref, wu_ref, k_ref, wd_ref, g_ref, b_ref, out_ref, st_ref,
              xb_ref, a_ref, carry_ref, acc_ref, *, tm, tiles_per_seq):
    @pl.when((pl.program_id(0) % tiles_per_seq) == 0)
    def _():
        carry_ref[...] = jnp.zeros(carry_ref.shape, F32)

    xb_ref[...] = x_ref[...].astype(BF16)

    def up(c):
        a = _mm(xb_ref[...], wu_ref[:, c * FF_CHUNK:(c + 1) * FF_CHUNK])
        gate = _mm(xb_ref[...], wu_ref[:, D_FF + c * FF_CHUNK:D_FF + (c + 1) * FF_CHUNK])
        return a, gate

    n_chunks = D_FF // FF_CHUNK
    nxt = up(0)
    for c in range(n_chunks):
        cs = slice(c * FF_CHUNK, (c + 1) * FF_CHUNK)
        buf = a_ref.at[c % 2]
        a, gate = nxt
        if c + 1 < n_chunks:
            nxt = up(c + 1)
        buf[0:HALO, :] = carry_ref[:, cs]
        buf[HALO:HALO + tm, :] = a
        a1 = buf[HALO - 1:HALO - 1 + tm, :]
        a2 = buf[HALO - 2:HALO - 2 + tm, :]
        carry_ref[:, cs] = buf[tm:tm + HALO, :]
        ac = a2 * k_ref[0:1, cs] + a1 * k_ref[1:2, cs] + a * k_ref[2:3, cs]
        hid = (_gelu(ac) * gate).astype(BF16)
        d = _mm(hid, wd_ref[cs, :])
        if c == 0:
            acc_ref[...] = d
        else:
            acc_ref[...] += d
    out_ref[...] = _layer_norm(ALPHA * x_ref[...] + acc_ref[...], g_ref[...], b_ref[...])
    st_ref[0] = carry_ref[HALO - (CONV_W - 1):HALO, :]


def _ffn_prompt(x, wu, k, wd, layer, ln_g, ln_b, batch, seq, tm):
    m = x.shape[0]
    tiles_per_seq = seq // tm
    act = pl.BlockSpec((tm, D_MODEL), lambda i: (i, 0))
    return pl.pallas_call(
        functools.partial(_ffn_body, tm=tm, tiles_per_seq=tiles_per_seq),
        out_shape=(jax.ShapeDtypeStruct((m, D_MODEL), F32),
                   jax.ShapeDtypeStruct((batch, CONV_W - 1, D_FF), F32)),
        grid=(m // tm,),
        in_specs=[act, _layer_spec((D_MODEL, 2 * D_FF), layer, resident=True), _const_spec((CONV_W, D_FF)),
                  _layer_spec((D_FF, D_MODEL), layer, resident=True),
                  _const_spec((1, D_MODEL)), _const_spec((1, D_MODEL))],
        out_specs=(act, pl.BlockSpec((1, CONV_W - 1, D_FF), lambda i: (i // tiles_per_seq, 0, 0))),
        scratch_shapes=[pltpu.VMEM((tm, D_MODEL), BF16),
                        pltpu.VMEM((2, HALO + tm, FF_CHUNK), F32),
                        pltpu.VMEM((HALO, D_FF), F32),
                        pltpu.VMEM((tm, D_MODEL), F32)],
        compiler_params=_cparams(("arbitrary",)),
        name="conv_ffn",
    )(x, wu, k, wd, ln_g, ln_b)


def _ffn_step_body(x_ref, st_ref, wu_ref, k_ref, wd_ref, g_ref, b_ref, out_ref, nst_ref):
    xb = x_ref[...].astype(BF16)
    a = _mm(xb, wu_ref[:, 0:D_FF])
    gate = _mm(xb, wu_ref[:, D_FF:2 * D_FF])
    a2 = st_ref[:, 0:D_FF]
    a1 = st_ref[:, D_FF:2 * D_FF]
    ac = a2 * k_ref[0:1, :] + a1 * k_ref[1:2, :] + a * k_ref[2:3, :]
    hid = (_gelu(ac) * gate).astype(BF16)
    f = _mm(hid, wd_ref[...])
    out_ref[...] = _layer_norm(ALPHA * x_ref[...] + f, g_ref[...], b_ref[...])
    nst_ref[:, 0:D_FF] = a1
    nst_ref[:, D_FF:2 * D_FF] = a


def _ffn_step(x, st, wu, k, wd, layer, ln_g, ln_b):
    nb = x.shape[0]
    st2 = st.reshape(nb, (CONV_W - 1) * D_FF)
    out, nst = pl.pallas_call(
        _ffn_step_body,
        out_shape=(jax.ShapeDtypeStruct(x.shape, F32), jax.ShapeDtypeStruct(st2.shape, F32)),
        grid=(1,),
        in_specs=[_const_spec(x.shape), _const_spec(st2.shape), _layer_spec(wu.shape[1:], layer),
                  _const_spec(k.shape), _layer_spec(wd.shape[1:], layer),
                  _const_spec(ln_g.shape), _const_spec(ln_b.shape)],
        out_specs=(_const_spec(x.shape), _const_spec(st2.shape)),
        compiler_params=_cparams(("arbitrary",)),
        name="conv_ffn_step",
    )(x, st2, wu, k, wd, ln_g, ln_b)
    return out, nst.reshape(st.shape)


def _pool_body(x_ref, win_ref, wg_ref, sc_ref, wout_ref, g_ref, b_ref, out_ref, st_ref,
               u_ref, *, tm, tiles_per_seq):
    _load_halo(u_ref, tm, tiles_per_seq)
    xb = x_ref[...].astype(BF16)
    u_ref[HALO:HALO + tm, :] = _mm(xb, win_ref[...])
    pos = (pl.program_id(0) % tiles_per_seq) * tm + lax.broadcasted_iota(jnp.int32, (tm, 1), 0)
    zs = []
    for gi, w in enumerate(POOL_WINDOWS):
        cs = slice(gi * POOL_GROUP, (gi + 1) * POOL_GROUP)
        u = u_ref[HALO:HALO + tm, cs]
        win_sum = u
        for j in range(1, w):
            win_sum = win_sum + u_ref[HALO - j:HALO - j + tm, cs]
        cnt = jnp.minimum(w, pos + 1).astype(F32)
        pooled = win_sum * (1.0 / cnt) - u
        z = _mm(pooled.astype(BF16), wg_ref[gi])
        zs.append(z * sc_ref[:, cs])
    zc = jnp.concatenate(zs, axis=1).astype(BF16)
    mix = _mm(zc, wout_ref[...])
    out_ref[...] = _layer_norm(ALPHA * x_ref[...] + mix, g_ref[...], b_ref[...])
    st_ref[0] = u_ref[HALO + tm - POOL_PAST:HALO + tm, :]


def _pool_prompt(x, win, wg, sc, wout, ln_g, ln_b, batch, seq, tm):
    m = x.shape[0]
    tiles_per_seq = seq // tm
    act = pl.BlockSpec((tm, D_MODEL), lambda i: (i, 0))
    return pl.pallas_call(
        functools.partial(_pool_body, tm=tm, tiles_per_seq=tiles_per_seq),
        out_shape=(jax.ShapeDtypeStruct((m, D_MODEL), F32),
                   jax.ShapeDtypeStruct((batch, POOL_PAST, D_MODEL), F32)),
        grid=(m // tm,),
        in_specs=[act, _const_spec(win.shape), _const_spec(wg.shape), _const_spec(sc.shape),
                  _const_spec(wout.shape), _const_spec((1, D_MODEL)), _const_spec((1, D_MODEL))],
        out_specs=(act, pl.BlockSpec((1, POOL_PAST, D_MODEL), lambda i: (i // tiles_per_seq, 0, 0))),
        scratch_shapes=[pltpu.VMEM((HALO + tm, D_MODEL), F32)],
        compiler_params=_cparams(("arbitrary",)),
        name="pool_mixer",
    )(x, win, wg, sc, wout, ln_g, ln_b)


def _pool_step_body(x_ref, st_ref, win_ref, wg_ref, sc_ref, wout_ref, g_ref, b_ref,
                    out_ref, nst_ref):
    xb = x_ref[...].astype(BF16)
    u_all = _mm(xb, win_ref[...])
    zs = []
    for gi, w in enumerate(POOL_WINDOWS):
        u = u_all[:, gi * POOL_GROUP:(gi + 1) * POOL_GROUP]
        win_sum = u
        for j in range(1, w):
            base = (POOL_PAST - j) * D_MODEL + gi * POOL_GROUP
            win_sum = win_sum + st_ref[:, base:base + POOL_GROUP]
        cnt = float(min(w, PAST_LEN + 1))
        pooled = win_sum * (1.0 / cnt) - u
        z = _mm(pooled.astype(BF16), wg_ref[gi])
        zs.append(z * sc_ref[:, gi * POOL_GROUP:(gi + 1) * POOL_GROUP])
    zc = jnp.concatenate(zs, axis=1).astype(BF16)
    mix = _mm(zc, wout_ref[...])
    out_ref[...] = _layer_norm(ALPHA * x_ref[...] + mix, g_ref[...], b_ref[...])
    keep = (POOL_PAST - 1) * D_MODEL
    nst_ref[:, 0:keep] = st_ref[:, D_MODEL:POOL_PAST * D_MODEL]
    nst_ref[:, keep:keep + D_MODEL] = u_all


def _pool_step(x, st, win, wg, sc, wout, ln_g, ln_b):
    nb = x.shape[0]
    st2 = st.reshape(nb, POOL_PAST * D_MODEL)
    ins = (x, st2, win, wg, sc, wout, ln_g, ln_b)
    out, nst = pl.pallas_call(
        _pool_step_body,
        out_shape=(jax.ShapeDtypeStruct(x.shape, F32), jax.ShapeDtypeStruct(st2.shape, F32)),
        grid=(1,),
        in_specs=[_const_spec(a.shape) for a in ins],
        out_specs=(_const_spec(x.shape), _const_spec(st2.shape)),
        compiler_params=_cparams(("arbitrary",)),
        name="pool_mixer_step",
    )(*ins)
    return out, nst.reshape(st.shape)


def _sconv_body(x_ref, win_ref, k_ref, wout_ref, g_ref, b_ref, out_ref, st_ref,
                z_ref, *, tm, tiles_per_seq):
    _load_halo(z_ref, tm, tiles_per_seq)
    xb = x_ref[...].astype(BF16)
    gb = _mm(xb, win_ref[:, 0:D_MODEL])
    gc = _mm(xb, win_ref[:, D_MODEL:2 * D_MODEL])
    hh = _mm(xb, win_ref[:, 2 * D_MODEL:3 * D_MODEL])
    z = gc * hh
    z_ref[HALO:HALO + tm, :] = z
    z1 = z_ref[HALO - 1:HALO - 1 + tm, :]
    z2 = z_ref[HALO - 2:HALO - 2 + tm, :]
    zc = z2 * k_ref[0:1, :] + z1 * k_ref[1:2, :] + z * k_ref[2:3, :]
    mix = _mm((gb * zc).astype(BF16), wout_ref[...])
    out_ref[...] = _layer_norm(ALPHA * x_ref[...] + mix, g_ref[...], b_ref[...])
    st_ref[0] = z_ref[HALO + tm - (CONV_W - 1):HALO + tm, :]


def _sconv_prompt(x, win, k, wout, ln_g, ln_b, batch, seq, tm):
    m = x.shape[0]
    tiles_per_seq = seq // tm
    act = pl.BlockSpec((tm, D_MODEL), lambda i: (i, 0))
    return pl.pallas_call(
        functools.partial(_sconv_body, tm=tm, tiles_per_seq=tiles_per_seq),
        out_shape=(jax.ShapeDtypeStruct((m, D_MODEL), F32),
                   jax.ShapeDtypeStruct((batch, CONV_W - 1, D_MODEL), F32)),
        grid=(m // tm,),
        in_specs=[act, _const_spec(win.shape), _const_spec(k.shape), _const_spec(wout.shape),
                  _const_spec((1, D_MODEL)), _const_spec((1, D_MODEL))],
        out_specs=(act, pl.BlockSpec((1, CONV_W - 1, D_MODEL), lambda i: (i // tiles_per_seq, 0, 0))),
        scratch_shapes=[pltpu.VMEM((HALO + tm, D_MODEL), F32)],
        compiler_params=_cparams(("arbitrary",)),
        name="sconv_mixer",
    )(x, win, k, wout, ln_g, ln_b)


def _sconv_step_body(x_ref, st_ref, win_ref, k_ref, wout_ref, g_ref, b_ref, out_ref, nst_ref):
    xb = x_ref[...].astype(BF16)
    gb = _mm(xb, win_ref[:, 0:D_MODEL])
    gc = _mm(xb, win_ref[:, D_MODEL:2 * D_MODEL])
    hh = _mm(xb, win_ref[:, 2 * D_MODEL:3 * D_MODEL])
    z = gc * hh
    z2 = st_ref[:, 0:D_MODEL]
    z1 = st_ref[:, D_MODEL:2 * D_MODEL]
    zc = z2 * k_ref[0:1, :] + z1 * k_ref[1:2, :] + z * k_ref[2:3, :]
    mix = _mm((gb * zc).astype(BF16), wout_ref[...])
    out_ref[...] = _layer_norm(ALPHA * x_ref[...] + mix, g_ref[...], b_ref[...])
    nst_ref[:, 0:D_MODEL] = z1
    nst_ref[:, D_MODEL:2 * D_MODEL] = z


def _sconv_step(x, st, win, k, wout, ln_g, ln_b):
    nb = x.shape[0]
    st2 = st.reshape(nb, (CONV_W - 1) * D_MODEL)
    ins = (x, st2, win, k, wout, ln_g, ln_b)
    out, nst = pl.pallas_call(
        _sconv_step_body,
        out_shape=(jax.ShapeDtypeStruct(x.shape, F32), jax.ShapeDtypeStruct(st2.shape, F32)),
        grid=(1,),
        in_specs=[_const_spec(a.shape) for a in ins],
        out_specs=(_const_spec(x.shape), _const_spec(st2.shape)),
        compiler_params=_cparams(("arbitrary",)),
        name="sconv_mixer_step",
    )(*ins)
    return out, nst.reshape(st.shape)


def kernel(x_prompt, x_sample, cache_kv_w128, cache_kv_w512, cache_kv_w2048, state_pool, state_sconv, state_ffn_conv, attn_w_qkv, attn_w_o, pool_w_in, pool_w_grp, pool_scale, pool_w_out, sconv_w_in, sconv_k, sconv_w_out, ffn_w_up, ffn_k, ffn_w_down, ln_g, ln_b):
    batch, seq, _ = x_prompt.shape
    nb = x_sample.shape[0]
    caches = (cache_kv_w128, cache_kv_w512, cache_kv_w2048)

    w_qkv = attn_w_qkv.astype(BF16)
    w_o = attn_w_o.astype(BF16)
    p_in = pool_w_in.astype(BF16)
    p_grp = pool_w_grp.astype(BF16)
    p_out = pool_w_out.astype(BF16)
    s_in = sconv_w_in.astype(BF16)
    s_out = sconv_w_out.astype(BF16)
    f_up = ffn_w_up.astype(BF16)
    f_down = ffn_w_down.astype(BF16)

    tm_qkv = 1024
    tables = [_rope_tables(jnp.arange(seq).reshape(seq // tm_qkv, tm_qkv // dil, dil)
                           .transpose(0, 2, 1).reshape(seq)) for _, dil in ATTN_GROUPS]
    cos_p = [t[0] for t in tables]
    sin_p = [t[1] for t in tables]
    cos_s, sin_s = _rope_tables(jnp.full((nb,), PAST_LEN))

    xp = x_prompt.reshape(batch * seq, D_MODEL)
    xs = x_sample.reshape(nb, D_MODEL)

    caches_t = [c.transpose(0, 1, 3, 4, 5, 2).reshape(c.shape[0], c.shape[1], 2, D_MODEL, c.shape[2])
                for c in caches]
    n_attn = caches[0].shape[0]
    kv_p = [None for _ in ATTN_GROUPS]
    kv_s = [None for _ in ATTN_GROUPS]
    pool_p, pool_s, sconv_p, sconv_s, ffn_p, ffn_s = [], [], [], [], [], []
    ia = ib = ic = 0
    tm_ffn = 1024
    tm = 512
    for i in range(DEPTH):
        kind = i % N_MIXERS
        g1 = ln_g[i, 0].reshape(1, D_MODEL)
        b1 = ln_b[i, 0].reshape(1, D_MODEL)
        g2 = ln_g[i, 1].reshape(1, D_MODEL)
        b2 = ln_b[i, 1].reshape(1, D_MODEL)
        if kind == 0:
            x3 = xp.reshape(batch, seq, D_MODEL)
            outs_p, lses_p, outs_s, lses_s = [], [], [], []
            for g, (win, dil) in enumerate(ATTN_GROUPS):
                n = seq // dil
                qkv_g = _qkv_rope(x3, w_qkv, ia, g, dil, cos_p[g], sin_p[g], tm_qkv, BF16,
                                  q_scale=ATTN_SCALE * LOG2E)
                o, l = _attn_prompt(qkv_g.reshape(batch * dil, n, 3 * D_MODEL), g)
                outs_p.append(o.reshape(batch, dil, n, D_MODEL))
                lses_p.append(l.reshape(batch, dil, n, LANES))
                kv_p[g] = _kv_tail(x3, w_qkv, cos_p[0], sin_p[0], kv_p[g], ia, n_attn, g, min(win, seq))
                qkv_s = _qkv_rope(xs.reshape(1, nb, D_MODEL), w_qkv, ia, g, 1, cos_s, sin_s, nb, F32)
                kv_s[g], o, l = _cache_step(qkv_s.reshape(nb, 3 * D_MODEL), caches_t[g], kv_s[g], ia, g, dil)
                outs_s.append(o.reshape(nb, D_MODEL))
                lses_s.append(l.reshape(nb, LANES))
            xp = _merge_wo_ln(outs_p, lses_p, x3, w_o[ia], g1, b1, tm).reshape(batch * seq, D_MODEL)
            xs = _merge_step(outs_s, lses_s, xs, w_o[ia], g1, b1)
            ia += 1
        elif kind == 1:
            sc = pool_scale[ib].reshape(1, D_MODEL)
            xp, st = _pool_prompt(xp, p_in[ib], p_grp[ib], sc, p_out[ib], g1, b1, batch, seq, tm)
            pool_p.append(st)
            xs, st = _pool_step(xs, state_pool[ib], p_in[ib], p_grp[ib], sc, p_out[ib], g1, b1)
            pool_s.append(st)
            ib += 1
        else:
            xp, st = _sconv_prompt(xp, s_in[ic], sconv_k[ic], s_out[ic], g1, b1, batch, seq, tm)
            sconv_p.append(st)
            xs, st = _sconv_step(xs, state_sconv[ic], s_in[ic], sconv_k[ic], s_out[ic], g1, b1)
            sconv_s.append(st)
            ic += 1
        xp, st = _ffn_prompt(xp, f_up, ffn_k[i], f_down, i, g2, b2, batch, seq, tm_ffn)
        ffn_p.append(st)
        xs, st = _ffn_step(xs, state_ffn_conv[i], f_up, ffn_k[i], f_down, i, g2, b2)
        ffn_s.append(st)

    y_prompt = xp.reshape(batch, seq, D_MODEL)
    y_sample = xs.reshape(nb, 1, D_MODEL)

    def time_major(t, rows):
        t = t.reshape(n_attn, rows, 2, N_HEADS, HEAD_DIM, t.shape[-1])
        return t.transpose(0, 1, 5, 2, 3, 4)

    return (y_prompt, y_sample,
            time_major(kv_p[0], batch), time_major(kv_s[0], nb),
            time_major(kv_p[1], batch), time_major(kv_s[1], nb),
            time_major(kv_p[2], batch), time_major(kv_s[2], nb),
            jnp.stack(pool_p), jnp.stack(pool_s),
            jnp.stack(sconv_p), jnp.stack(sconv_s),
            jnp.stack(ffn_p), jnp.stack(ffn_s))
```

```python
import functools

import jax
import jax.numpy as jnp
from jax import lax
from jax.experimental import pallas as pl
from jax.experimental.pallas import tpu as pltpu

F32 = jnp.float32
BF16 = jnp.bfloat16

D_MODEL = 1024
DEPTH = 4
PAST_LEN = 8192
N_MIXERS = 3
ATTN_GROUPS = ((128, 1), (512, 4), (2048, 16))
N_GROUPS = len(ATTN_GROUPS)
HEAD_DIM = 64
N_HEADS = D_MODEL // HEAD_DIM
ATTN_SCALE = HEAD_DIM ** -0.5
ROPE_THETA = 10000.0
POOL_WINDOWS = (2, 4, 8, 16)
POOL_GROUP = D_MODEL // len(POOL_WINDOWS)
POOL_PAST = max(POOL_WINDOWS) - 1
CONV_W = 3
D_FF = ((8 * D_MODEL // 3 + 127) // 128) * 128
ALPHA = (2.0 * DEPTH) ** 0.25
LN_EPS = 1e-5
QKV_COLS = N_GROUPS * 3 * D_MODEL

LANES = 128
HALO = 16
ATTN_BLOCK = 128
FF_CHUNK = 256
CACHE_BLOCK_BYTES = 8 * 1024 * 1024
KV_TAIL_TILE = 512
QKV_ROW_CHUNK = 128
SPLIT_STRIDE = 4
ATTN_STEP_ROWS = 512
VMEM_LIMIT = 56 * 1024 * 1024
NEG = -1e30
LOG2E = 1.4426950408889634
LN2 = 0.6931471805599453


def _cparams(sem):
    return pltpu.CompilerParams(dimension_semantics=sem, vmem_limit_bytes=VMEM_LIMIT)


def _layer_norm(z, g, b):
    mu = jnp.mean(z, axis=-1, keepdims=True)
    zc = z - mu
    var = jnp.mean(zc * zc, axis=-1, keepdims=True)
    return zc * lax.rsqrt(var + LN_EPS) * g + b


def _mm(a, w):
    return jnp.dot(a, w, preferred_element_type=F32)


def _gelu(x):
    return 0.5 * x * (1.0 + lax.erf(x * (2.0 ** -0.5)))


def _const_spec(shape):
    nd = len(shape)
    return pl.BlockSpec(shape, lambda *_: (0,) * nd)


def _layer_spec(shape, layer, resident=False):
    nd = len(shape)
    mode = pl.Buffered(1) if resident else None
    return pl.BlockSpec((None,) + tuple(shape), lambda *_: (layer,) + (0,) * nd, pipeline_mode=mode)


def _qkv_body(x_ref, w_ref, cos_ref, sin_ref, o_ref, xb_ref, slab_ref, *, chunk, q_scale, dil):
    tm = x_ref.shape[1]
    n = tm // dil
    piece = min(chunk, n)
    pre = SPLIT_STRIDE if dil > SPLIT_STRIDE else 1
    step = dil // pre
    if dil == 1:
        xb_ref[...] = x_ref[0].astype(BF16)
    else:
        for c in range(D_MODEL // LANES):
            slab_ref[0, c] = x_ref[0, :, c * LANES:(c + 1) * LANES]
        if pre > 1:
            for r1 in range(pre):
                for c in range(D_MODEL // LANES):
                    slab_ref[1, c, r1 * (tm // pre):(r1 + 1) * (tm // pre), :] = (
                        slab_ref[0, c, pl.ds(r1, tm // pre, stride=pre), :])
    src_slab = slab_ref.at[1 if pre > 1 else 0]

    def regroup(rc):
        for p in range(chunk // piece):
            off = rc * chunk + p * piece
            r, i0 = off // n, off % n
            start = (r % pre) * (tm // pre) + r // pre + i0 * step
            src = pl.ds(start, piece, stride=step)
            for c in range(D_MODEL // LANES):
                xb_ref[off:off + piece, c * LANES:(c + 1) * LANES] = src_slab[c, src, :].astype(BF16)

    lane = lax.broadcasted_iota(jnp.int32, (chunk, LANES), 1)
    first_half = (lane & (HEAD_DIM // 2)) == 0

    def store(rc, col, val):
        for p in range(chunk // piece):
            off = rc * chunk + p * piece
            o_ref[0, off // n, off % n:off % n + piece, col:col + val.shape[1]] = (
                val[p * piece:(p + 1) * piece].astype(o_ref.dtype))

    for unit in range(3):
        for rc in range(tm // chunk):
            rs = slice(rc * chunk, (rc + 1) * chunk)
            if unit == 0 and dil > 1:
                regroup(rc)
            y = _mm(xb_ref[rs, :], w_ref[:, unit * D_MODEL:(unit + 1) * D_MODEL])
            if unit == 2:
                store(rc, unit * D_MODEL, y)
                continue
            cos = cos_ref[rs, :]
            sin = sin_ref[rs, :]
            if unit == 0 and q_scale != 1.0:
                cos = cos * q_scale
                sin = sin * q_scale
            for c in range(D_MODEL // LANES):
                yc = y[:, c * LANES:(c + 1) * LANES]
                partner = jnp.where(first_half,
                                    pltpu.roll(yc, LANES - HEAD_DIM // 2, 1),
                                    pltpu.roll(yc, HEAD_DIM // 2, 1))
                store(rc, unit * D_MODEL + c * LANES, yc * cos + partner * sin)


def _qkv_rope(x3, w, layer, g, dil, cos, sin, tm, out_dtype, q_scale=1.0):
    nseq, seq, _ = x3.shape
    table = pl.BlockSpec((tm, LANES), lambda b, i: (i, 0))
    return pl.pallas_call(
        functools.partial(_qkv_body, chunk=min(tm, QKV_ROW_CHUNK), q_scale=q_scale, dil=dil),
        out_shape=jax.ShapeDtypeStruct((nseq, dil, seq // dil, 3 * D_MODEL), out_dtype),
        grid=(nseq, seq // tm),
        in_specs=[
            pl.BlockSpec((1, tm, D_MODEL), lambda b, i: (b, i, 0)),
            pl.BlockSpec((None, D_MODEL, 3 * D_MODEL), lambda b, i: (layer, 0, g)),
            table, table,
        ],
        out_specs=pl.BlockSpec((1, dil, tm // dil, 3 * D_MODEL), lambda b, i: (b, 0, i, 0)),
        scratch_shapes=[pltpu.VMEM((tm, D_MODEL), BF16),
                        pltpu.VMEM((2 if dil > SPLIT_STRIDE else 1, D_MODEL // LANES,
                                    tm if dil > 1 else 8, LANES), F32)],
        compiler_params=_cparams(("arbitrary", "arbitrary")),
        name=f"qkv_rope_g{g}",
    )(x3, w, cos, sin)


def _store_classes(slab_ref, src_ref, dil):
    n = src_ref.shape[2]
    for r in range(dil):
        for c in range(src_ref.shape[3] // LANES):
            slab_ref[c, pl.ds(r, n, stride=dil), :] = src_ref[0, r, :, c * LANES:(c + 1) * LANES]


def _rope_tables(pos):
    half = HEAD_DIM // 2
    inv = ROPE_THETA ** (-jnp.arange(half, dtype=F32) / half)
    ang = pos.astype(F32)[:, None] * inv[None, :]
    cos = jnp.cos(ang)
    sin = jnp.sin(ang)
    reps = LANES // HEAD_DIM
    cos_t = jnp.tile(jnp.concatenate([cos, cos], axis=1), (1, reps))
    sin_t = jnp.tile(jnp.concatenate([-sin, sin], axis=1), (1, reps))
    return cos_t, sin_t


def _band_mask(has_prev):
    tq = ATTN_BLOCK
    row = lax.broadcasted_iota(jnp.int32, (tq, 2 * tq), 0)
    col = lax.broadcasted_iota(jnp.int32, (tq, 2 * tq), 1)
    delta = row + tq - col
    return (delta >= 0) & (delta <= tq) & ((col >= tq) | has_prev)


def _attn_pair(qp, k2, v2, valid, low):
    outs, lses = [], []
    for half in range(2):
        sel = low if half == 0 else jnp.logical_not(low)
        qm = jnp.where(sel, qp, 0.0)
        s = lax.dot_general(qm, k2, (((1,), (1,)), ((), ())), preferred_element_type=F32)
        s = jnp.where(valid, s, NEG)
        m = jnp.max(s, axis=1, keepdims=True)
        p = jnp.exp2(s - m)
        l = jnp.sum(p, axis=1, keepdims=True)
        pv = _mm(p.astype(BF16), v2)
        outs.append(pv * (1.0 / l))
        lses.append((m + jnp.log2(l)) * LN2)
    return jnp.where(low, outs[0], outs[1]), lses[0], lses[1]


def _attn_body(q_ref, kp_ref, kc_ref, vp_ref, vc_ref, o_ref, lse_ref, k_ref, v_ref):
    rows = q_ref.shape[1]
    not_first = pl.program_id(1) > 0
    k_ref[0:ATTN_BLOCK, :] = kp_ref[0]
    k_ref[ATTN_BLOCK:ATTN_BLOCK + rows, :] = kc_ref[0]
    v_ref[0:ATTN_BLOCK, :] = vp_ref[0]
    v_ref[ATTN_BLOCK:ATTN_BLOCK + rows, :] = vc_ref[0]
    lane = lax.broadcasted_iota(jnp.int32, (ATTN_BLOCK, LANES), 1)
    low = lane < HEAD_DIM
    for sb in range(rows // ATTN_BLOCK):
        valid = _band_mask(True if sb > 0 else not_first)
        qs = slice(sb * ATTN_BLOCK, (sb + 1) * ATTN_BLOCK)
        ks = slice(sb * ATTN_BLOCK, (sb + 2) * ATTN_BLOCK)
        lse_full = jnp.zeros((ATTN_BLOCK, LANES), F32)
        for hp in range(D_MODEL // LANES):
            cs = slice(hp * LANES, (hp + 1) * LANES)
            o, lse0, lse1 = _attn_pair(q_ref[0, qs, cs], k_ref[ks, cs], v_ref[ks, cs], valid, low)
            o_ref[0, qs, cs] = o
            lse_full = jnp.where(lane == 2 * hp, lse0, lse_full)
            lse_full = jnp.where(lane == 2 * hp + 1, lse1, lse_full)
        lse_ref[0, qs, :] = lse_full


def _attn_prompt(qkv3, g):
    nseq, n, _ = qkv3.shape
    rows = min(n, ATTN_STEP_ROWS)
    per_step = rows // ATTN_BLOCK
    blk = (1, rows, D_MODEL)

    def cur(unit):
        return pl.BlockSpec(blk, lambda b, i: (b, i, unit))

    def prev(unit):
        return pl.BlockSpec((1, ATTN_BLOCK, D_MODEL),
                            lambda b, i: (b, jnp.maximum(i * per_step - 1, 0), unit))

    return pl.pallas_call(
        _attn_body,
        out_shape=(jax.ShapeDtypeStruct((nseq, n, D_MODEL), F32),
                   jax.ShapeDtypeStruct((nseq, n, LANES), F32)),
        grid=(nseq, n // rows),
        in_specs=[cur(0), prev(1), cur(1), prev(2), cur(2)],
        out_specs=(pl.BlockSpec(blk, lambda b, i: (b, i, 0)),
                   pl.BlockSpec((1, rows, LANES), lambda b, i: (b, i, 0))),
        scratch_shapes=[pltpu.VMEM((ATTN_BLOCK + rows, D_MODEL), BF16),
                        pltpu.VMEM((ATTN_BLOCK + rows, D_MODEL), BF16)],
        compiler_params=_cparams(("arbitrary", "arbitrary")),
        name=f"attn_prompt_g{g}",
    )(qkv3, qkv3, qkv3, qkv3, qkv3)


def _cache_step_body(*refs, heads, win, dil, aliased):
    q_ref, knr_ref, vnr_ref, knt_ref, vnt_ref, c_ref = refs[:6]
    nc_ref, o_ref, lse_ref = refs[-3:]
    b = pl.program_id(0)
    hc = pl.program_id(1)
    rows = heads * HEAD_DIM
    sub = 16
    rowh = lax.broadcasted_iota(jnp.int32, (sub, rows), 0)
    own = rowh == lax.broadcasted_iota(jnp.int32, (sub, rows), 1) // HEAD_DIM
    q = q_ref[0] * ATTN_SCALE
    qbd = jnp.where(own, jnp.broadcast_to(q, (sub, rows)), 0.0).astype(BF16)
    k = c_ref[0]
    v = c_ref[1]
    t = lax.broadcasted_iota(jnp.int32, (sub, win), 1)
    s = jnp.where((t & (dil - 1)) == 0, _mm(qbd, k.astype(BF16)), NEG)
    knr = knr_ref[0].astype(BF16).astype(F32)
    vnr = vnr_ref[0].astype(BF16).astype(F32)
    s_new = jnp.sum(qbd.astype(F32) * knr, axis=1, keepdims=True)
    m = jnp.maximum(jnp.max(s, axis=1, keepdims=True), s_new)
    p = jnp.exp(s - m)
    p_new = jnp.exp(s_new - m)
    l = jnp.sum(p, axis=1, keepdims=True) + p_new
    pv = lax.dot_general(p.astype(BF16), v.astype(BF16), (((1,), (1,)), ((), ())),
                         preferred_element_type=F32)
    o16 = (pv + p_new.astype(BF16).astype(F32) * vnr) * (1.0 / l)
    o_ref[0] = jnp.sum(jnp.where(own, o16, 0.0), axis=0, keepdims=True)
    lse = m + jnp.log(l)
    r2 = lax.broadcasted_iota(jnp.int32, (sub, LANES), 0)
    l2 = lax.broadcasted_iota(jnp.int32, (sub, LANES), 1)
    lse_row = jnp.sum(jnp.where((r2 < heads) & (r2 + hc * heads == l2), lse, 0.0), axis=0, keepdims=True)

    @pl.when(hc == 0)
    def _():
        lse_ref[0] = lse_row

    @pl.when(hc > 0)
    def _():
        lse_ref[0] = lse_ref[0] + lse_row

    pick = lax.broadcasted_iota(jnp.int32, knt_ref.shape, 1) == b
    kn_col = jnp.sum(jnp.where(pick, knt_ref[...], 0.0), axis=1, keepdims=True)
    vn_col = jnp.sum(jnp.where(pick, vnt_ref[...], 0.0), axis=1, keepdims=True)
    last = lax.broadcasted_iota(jnp.int32, (rows, win), 1) == win - 1
    nc_ref[0] = jnp.where(last, kn_col, pltpu.roll(k, win - 1, 1))
    nc_ref[1] = jnp.where(last, vn_col, pltpu.roll(v, win - 1, 1))


def _cache_step(qkv_s, cache_t, prev_out, layer, g, dil):
    n_layers, nb, _, _, win = cache_t.shape
    heads = max(1, min(N_HEADS, CACHE_BLOCK_BYTES // (2 * HEAD_DIM * win * 4)))
    rows = heads * HEAD_DIM
    per_unit = D_MODEL // rows
    aliased = prev_out is not None
    qkv_s3 = qkv_s.reshape(nb, 1, 3 * D_MODEL)
    qkv_t = qkv_s.T

    def row(u):
        return pl.BlockSpec((1, 1, rows), lambda b, hc: (b, 0, u * per_unit + hc))

    def col(u):
        return pl.BlockSpec((rows, nb), lambda b, hc: (u * per_unit + hc, 0))

    cblk = (None, None, 2, rows, win)
    in_specs = [row(0), row(1), row(2), col(1), col(2),
                pl.BlockSpec(cblk, lambda b, hc: (layer, b, 0, hc, 0))]
    args = [qkv_s3, qkv_s3, qkv_s3, qkv_t, qkv_t, cache_t]
    if aliased:
        in_specs.append(pl.BlockSpec(memory_space=pl.ANY))
        args.append(prev_out)
    return pl.pallas_call(
        functools.partial(_cache_step_body, heads=heads, win=win, dil=dil, aliased=aliased),
        out_shape=(jax.ShapeDtypeStruct(cache_t.shape, F32),
                   jax.ShapeDtypeStruct((nb, 1, D_MODEL), F32),
                   jax.ShapeDtypeStruct((nb, 1, LANES), F32)),
        grid=(nb, N_HEADS // heads),
        in_specs=in_specs,
        out_specs=(pl.BlockSpec(cblk, lambda b, hc: (layer, b, 0, hc, 0)),
                   pl.BlockSpec((1, 1, rows), lambda b, hc: (b, 0, hc)),
                   pl.BlockSpec((1, 1, LANES), lambda b, hc: (b, 0, 0))),
        input_output_aliases={6: 0} if aliased else {},
        compiler_params=_cparams(("arbitrary", "arbitrary")),
        name=f"cache_step_g{g}",
    )(*args)


def _kv_tail_body(*refs):
    x_ref, wk_ref, wv_ref, cos_ref, sin_ref = refs[:5]
    o_ref = refs[-1]
    chunk = LANES
    lane = lax.broadcasted_iota(jnp.int32, (chunk, LANES), 1)
    first_half = (lane & (HEAD_DIM // 2)) == 0
    xb = x_ref[0].astype(BF16)
    for rc in range(x_ref.shape[1] // chunk):
        rs = slice(rc * chunk, (rc + 1) * chunk)
        cos = cos_ref[rs, :]
        sin = sin_ref[rs, :]
        yk = _mm(xb[rs], wk_ref[...])
        yv = _mm(xb[rs], wv_ref[...])
        for c in range(D_MODEL // LANES):
            cs = slice(c * LANES, (c + 1) * LANES)
            yc = yk[:, cs]
            partner = jnp.where(first_half,
                                pltpu.roll(yc, LANES - HEAD_DIM // 2, 1),
                                pltpu.roll(yc, HEAD_DIM // 2, 1))
            o_ref[0, cs, rs] = (yc * cos + partner * sin).T
            o_ref[1, cs, rs] = yv[:, cs].T


def _kv_tail(x3, w, cos, sin, prev_out, layer, n_layers, g, win):
    batch, seq, _ = x3.shape
    tt = min(win, KV_TAIL_TILE)
    first = (seq - win) // tt
    aliased = prev_out is not None
    table = pl.BlockSpec((tt, LANES), lambda b, i: (first + i, 0))
    in_specs = [pl.BlockSpec((1, tt, D_MODEL), lambda b, i: (b, first + i, 0)),
                pl.BlockSpec((None, D_MODEL, D_MODEL), lambda b, i: (layer, 0, 3 * g + 1)),
                pl.BlockSpec((None, D_MODEL, D_MODEL), lambda b, i: (layer, 0, 3 * g + 2)),
                table, table]
    args = [x3, w, w, cos, sin]
    if aliased:
        in_specs.append(pl.BlockSpec(memory_space=pl.ANY))
        args.append(prev_out)
    return pl.pallas_call(
        _kv_tail_body,
        out_shape=jax.ShapeDtypeStruct((n_layers, batch, 2, D_MODEL, win), F32),
        grid=(batch, win // tt),
        in_specs=in_specs,
        out_specs=pl.BlockSpec((None, None, 2, D_MODEL, tt), lambda b, i: (layer, b, 0, 0, i)),
        input_output_aliases={5: 0} if aliased else {},
        compiler_params=_cparams(("arbitrary", "arbitrary")),
        name=f"kv_tail_g{g}",
    )(*args)


def _group_weights(ls, expand):
    m = jnp.maximum(jnp.maximum(ls[0], ls[1]), ls[2])
    ex = [jnp.exp(l - m) for l in ls]
    inv_den = 1.0 / (ex[0] + ex[1] + ex[2])
    wfs = []
    for ex_g in ex:
        w = ex_g * inv_den
        w_hi = w.astype(BF16)
        w_lo = (w - w_hi.astype(F32)).astype(BF16)
        wfs.append(_mm(w_hi, expand) + _mm(w_lo, expand))
    return wfs


def _merge_step_body(o0_ref, o1_ref, o2_ref, l0_ref, l1_ref, l2_ref, x_ref, wo_ref, e_ref,
                     g_ref, b_ref, out_ref):
    wfs = _group_weights([l0_ref[...], l1_ref[...], l2_ref[...]], e_ref[...])
    merged = wfs[0] * o0_ref[...] + wfs[1] * o1_ref[...] + wfs[2] * o2_ref[...]
    mix = _mm(merged.astype(BF16), wo_ref[...])
    out_ref[...] = _layer_norm(ALPHA * x_ref[...] + mix, g_ref[...], b_ref[...])


def _head_expand():
    head = jnp.arange(LANES)[:, None]
    lane_head = jnp.arange(D_MODEL)[None, :] // HEAD_DIM
    return (head == lane_head).astype(BF16)


def _merge_step(outs, lses, x, wo, ln_g, ln_b):
    ins = (*outs, *lses, x, wo, _head_expand(), ln_g, ln_b)
    return pl.pallas_call(
        _merge_step_body,
        out_shape=jax.ShapeDtypeStruct(x.shape, F32),
        grid=(1,),
        in_specs=[_const_spec(a.shape) for a in ins],
        out_specs=_const_spec(x.shape),
        compiler_params=_cparams(("arbitrary",)),
        name="merge_wo_ln_step",
    )(*ins)


def _merge_body(o0_ref, o1_ref, o2_ref, l0_ref, l1_ref, l2_ref, x_ref, wo_ref, e_ref,
                g_ref, b_ref, out_ref, oslab_ref, lslab_ref, merged_ref):
    o_refs = (o0_ref, o1_ref, o2_ref)
    l_refs = (l0_ref, l1_ref, l2_ref)
    for gi, (_, dil) in enumerate(ATTN_GROUPS):
        _store_classes(oslab_ref.at[gi], o_refs[gi], dil)
        _store_classes(lslab_ref.at[gi], l_refs[gi], dil)
    wfs = _group_weights([lslab_ref[gi, 0] for gi in range(N_GROUPS)], e_ref[...])
    for c in range(D_MODEL // LANES):
        cs = slice(c * LANES, (c + 1) * LANES)
        t = wfs[0][:, cs] * oslab_ref[0, c] + wfs[1][:, cs] * oslab_ref[1, c] + wfs[2][:, cs] * oslab_ref[2, c]
        merged_ref[:, cs] = t.astype(BF16)
    mix = _mm(merged_ref[...], wo_ref[...])
    out_ref[0] = _layer_norm(ALPHA * x_ref[0] + mix, g_ref[...], b_ref[...])


def _merge_wo_ln(outs, lses, x3, wo, ln_g, ln_b, tm):
    batch, seq, _ = x3.shape
    expand = _head_expand()
    act = pl.BlockSpec((1, tm, D_MODEL), lambda b, i: (b, i, 0))

    def classes(dil, width):
        return pl.BlockSpec((1, dil, tm // dil, width), lambda b, i: (b, 0, i, 0))

    pairs = D_MODEL // LANES
    return pl.pallas_call(
        _merge_body,
        out_shape=jax.ShapeDtypeStruct((batch, seq, D_MODEL), F32),
        grid=(batch, seq // tm),
        in_specs=[classes(dil, D_MODEL) for _, dil in ATTN_GROUPS]
        + [classes(dil, LANES) for _, dil in ATTN_GROUPS]
        + [act, _const_spec((D_MODEL, D_MODEL)), _const_spec((LANES, D_MODEL)),
           _const_spec((1, D_MODEL)), _const_spec((1, D_MODEL))],
        out_specs=act,
        scratch_shapes=[pltpu.VMEM((N_GROUPS, pairs, tm, LANES), F32),
                        pltpu.VMEM((N_GROUPS, 1, tm, LANES), F32),
                        pltpu.VMEM((tm, D_MODEL), BF16)],
        compiler_params=_cparams(("arbitrary", "arbitrary")),
        name="merge_wo_ln",
    )(*outs, *lses, x3, wo, expand, ln_g, ln_b)


def _load_halo(ext_ref, tm, tiles_per_seq):
    first = (pl.program_id(0) % tiles_per_seq) == 0

    @pl.when(first)
    def _():
        ext_ref[0:HALO, :] = jnp.zeros((HALO, ext_ref.shape[1]), F32)

    @pl.when(jnp.logical_not(first))
    def _():
        ext_ref[0:HALO, :] = ext_ref[tm:tm + HALO, :]


def _ffn_body(x_ref, wu_ref, k_ref, wd_ref, g_ref, b_ref, out_ref, st_ref,
              xb_ref, a_ref, carry_ref, acc_ref, *, tm, tiles_per_seq):
    @pl.when((pl.program_id(0) % tiles_per_seq) == 0)
    def _():
        carry_ref[...] = jnp.zeros(carry_ref.shape, F32)

    xb_ref[...] = x_ref[...].astype(BF16)

    def up(c):
        a = _mm(xb_ref[...], wu_ref[:, c * FF_CHUNK:(c + 1) * FF_CHUNK])
        gate = _mm(xb_ref[...], wu_ref[:, D_FF + c * FF_CHUNK:D_FF + (c + 1) * FF_CHUNK])
        return a, gate

    n_chunks = D_FF // FF_CHUNK
    nxt = up(0)
    for c in range(n_chunks):
        cs = slice(c * FF_CHUNK, (c + 1) * FF_CHUNK)
        buf = a_ref.at[c % 2]
        a, gate = nxt
        if c + 1 < n_chunks:
            nxt = up(c + 1)
        buf[0:HALO, :] = carry_ref[:, cs]
        buf[HALO:HALO + tm, :] = a
        a1 = buf[HALO - 1:HALO - 1 + tm, :]
        a2 = buf[HALO - 2:HALO - 2 + tm, :]
        carry_ref[:, cs] = buf[tm:tm + HALO, :]
        ac = a2 * k_ref[0:1, cs] + a1 * k_ref[1:2, cs] + a * k_ref[2:3, cs]
        hid = (_gelu(ac) * gate).astype(BF16)
        d = _mm(hid, wd_ref[cs, :])
        if c == 0:
            acc_ref[...] = d
        else:
            acc_ref[...] += d
    out_ref[...] = _layer_norm(ALPHA * x_ref[...] + acc_ref[...], g_ref[...], b_ref[...])
    st_ref[0] = carry_ref[HALO - (CONV_W - 1):HALO, :]


def _ffn_prompt(x, wu, k, wd, layer, ln_g, ln_b, batch, seq, tm):
    m = x.shape[0]
    tiles_per_seq = seq // tm
    act = pl.BlockSpec((tm, D_MODEL), lambda i: (i, 0))
    return pl.pallas_call(
        functools.partial(_ffn_body, tm=tm, tiles_per_seq=tiles_per_seq),
        out_shape=(jax.ShapeDtypeStruct((m, D_MODEL), F32),
                   jax.ShapeDtypeStruct((batch, CONV_W - 1, D_FF), F32)),
        grid=(m // tm,),
        in_specs=[act, _layer_spec((D_MODEL, 2 * D_FF), layer, resident=True), _const_spec((CONV_W, D_FF)),
                  _layer_spec((D_FF, D_MODEL), layer, resident=True),
                  _const_spec((1, D_MODEL)), _const_spec((1, D_MODEL))],
        out_specs=(act, pl.BlockSpec((1, CONV_W - 1, D_FF), lambda i: (i // tiles_per_seq, 0, 0))),
        scratch_shapes=[pltpu.VMEM((tm, D_MODEL), BF16),
                        pltpu.VMEM((2, HALO + tm, FF_CHUNK), F32),
                        pltpu.VMEM((HALO, D_FF), F32),
                        pltpu.VMEM((tm, D_MODEL), F32)],
        compiler_params=_cparams(("arbitrary",)),
        name="conv_ffn",
    )(x, wu, k, wd, ln_g, ln_b)


def _ffn_step_body(x_ref, st_ref, wu_ref, k_ref, wd_ref, g_ref, b_ref, out_ref, nst_ref):
    xb = x_ref[...].astype(BF16)
    a = _mm(xb, wu_ref[:, 0:D_FF])
    gate = _mm(xb, wu_ref[:, D_FF:2 * D_FF])
    a2 = st_ref[:, 0:D_FF]
    a1 = st_ref[:, D_FF:2 * D_FF]
    ac = a2 * k_ref[0:1, :] + a1 * k_ref[1:2, :] + a * k_ref[2:3, :]
    hid = (_gelu(ac) * gate).astype(BF16)
    f = _mm(hid, wd_ref[...])
    out_ref[...] = _layer_norm(ALPHA * x_ref[...] + f, g_ref[...], b_ref[...])
    nst_ref[:, 0:D_FF] = a1
    nst_ref[:, D_FF:2 * D_FF] = a


def _ffn_step(x, st, wu, k, wd, layer, ln_g, ln_b):
    nb = x.shape[0]
    st2 = st.reshape(nb, (CONV_W - 1) * D_FF)
    out, nst = pl.pallas_call(
        _ffn_step_body,
        out_shape=(jax.ShapeDtypeStruct(x.shape, F32), jax.ShapeDtypeStruct(st2.shape, F32)),
        grid=(1,),
        in_specs=[_const_spec(x.shape), _const_spec(st2.shape), _layer_spec(wu.shape[1:], layer),
                  _const_spec(k.shape), _layer_spec(wd.shape[1:], layer),
                  _const_spec(ln_g.shape), _const_spec(ln_b.shape)],
        out_specs=(_const_spec(x.shape), _const_spec(st2.shape)),
        compiler_params=_cparams(("arbitrary",)),
        name="conv_ffn_step",
    )(x, st2, wu, k, wd, ln_g, ln_b)
    return out, nst.reshape(st.shape)


def _pool_body(x_ref, win_ref, wg_ref, sc_ref, wout_ref, g_ref, b_ref, out_ref, st_ref,
               u_ref, *, tm, tiles_per_seq):
    _load_halo(u_ref, tm, tiles_per_seq)
    xb = x_ref[...].astype(BF16)
    u_ref[HALO:HALO + tm, :] = _mm(xb, win_ref[...])
    pos = (pl.program_id(0) % tiles_per_seq) * tm + lax.broadcasted_iota(jnp.int32, (tm, 1), 0)
    zs = []
    for gi, w in enumerate(POOL_WINDOWS):
        cs = slice(gi * POOL_GROUP, (gi + 1) * POOL_GROUP)
        u = u_ref[HALO:HALO + tm, cs]
        win_sum = u
        for j in range(1, w):
            win_sum = win_sum + u_ref[HALO - j:HALO - j + tm, cs]
        cnt = jnp.minimum(w, pos + 1).astype(F32)
        pooled = win_sum * (1.0 / cnt) - u
        z = _mm(pooled.astype(BF16), wg_ref[gi])
        zs.append(z * sc_ref[:, cs])
    zc = jnp.concatenate(zs, axis=1).astype(BF16)
    mix = _mm(zc, wout_ref[...])
    out_ref[...] = _layer_norm(ALPHA * x_ref[...] + mix, g_ref[...], b_ref[...])
    st_ref[0] = u_ref[HALO + tm - POOL_PAST:HALO + tm, :]


def _pool_prompt(x, win, wg, sc, wout, ln_g, ln_b, batch, seq, tm):
    m = x.shape[0]
    tiles_per_seq = seq // tm
    act = pl.BlockSpec((tm, D_MODEL), lambda i: (i, 0))
    return pl.pallas_call(
        functools.partial(_pool_body, tm=tm, tiles_per_seq=tiles_per_seq),
        out_shape=(jax.ShapeDtypeStruct((m, D_MODEL), F32),
                   jax.ShapeDtypeStruct((batch, POOL_PAST, D_MODEL), F32)),
        grid=(m // tm,),
        in_specs=[act, _const_spec(win.shape), _const_spec(wg.shape), _const_spec(sc.shape),
                  _const_spec(wout.shape), _const_spec((1, D_MODEL)), _const_spec((1, D_MODEL))],
        out_specs=(act, pl.BlockSpec((1, POOL_PAST, D_MODEL), lambda i: (i // tiles_per_seq, 0, 0))),
        scratch_shapes=[pltpu.VMEM((HALO + tm, D_MODEL), F32)],
        compiler_params=_cparams(("arbitrary",)),
        name="pool_mixer",
    )(x, win, wg, sc, wout, ln_g, ln_b)


def _pool_step_body(x_ref, st_ref, win_ref, wg_ref, sc_ref, wout_ref, g_ref, b_ref,
                    out_ref, nst_ref):
    xb = x_ref[...].astype(BF16)
    u_all = _mm(xb, win_ref[...])
    zs = []
    for gi, w in enumerate(POOL_WINDOWS):
        u = u_all[:, gi * POOL_GROUP:(gi + 1) * POOL_GROUP]
        win_sum = u
        for j in range(1, w):
            base = (POOL_PAST - j) * D_MODEL + gi * POOL_GROUP
            win_sum = win_sum + st_ref[:, base:base + POOL_GROUP]
        cnt = float(min(w, PAST_LEN + 1))
        pooled = win_sum * (1.0 / cnt) - u
        z = _mm(pooled.astype(BF16), wg_ref[gi])
        zs.append(z * sc_ref[:, gi * POOL_GROUP:(gi + 1) * POOL_GROUP])
    zc = jnp.concatenate(zs, axis=1).astype(BF16)
    mix = _mm(zc, wout_ref[...])
    out_ref[...] = _layer_norm(ALPHA * x_ref[...] + mix, g_ref[...], b_ref[...])
    keep = (POOL_PAST - 1) * D_MODEL
    nst_ref[:, 0:keep] = st_ref[:, D_MODEL:POOL_PAST * D_MODEL]
    nst_ref[:, keep:keep + D_MODEL] = u_all


def _pool_step(x, st, win, wg, sc, wout, ln_g, ln_b):
    nb = x.shape[0]
    st2 = st.reshape(nb, POOL_PAST * D_MODEL)
    ins = (x, st2, win, wg, sc, wout, ln_g, ln_b)
    out, nst = pl.pallas_call(
        _pool_step_body,
        out_shape=(jax.ShapeDtypeStruct(x.shape, F32), jax.ShapeDtypeStruct(st2.shape, F32)),
        grid=(1,),
        in_specs=[_const_spec(a.shape) for a in ins],
        out_specs=(_const_spec(x.shape), _const_spec(st2.shape)),
        compiler_params=_cparams(("arbitrary",)),
        name="pool_mixer_step",
    )(*ins)
    return out, nst.reshape(st.shape)


def _sconv_body(x_ref, win_ref, k_ref, wout_ref, g_ref, b_ref, out_ref, st_ref,
                z_ref, *, tm, tiles_per_seq):
    _load_halo(z_ref, tm, tiles_per_seq)
    xb = x_ref[...].astype(BF16)
    gb = _mm(xb, win_ref[:, 0:D_MODEL])
    gc = _mm(xb, win_ref[:, D_MODEL:2 * D_MODEL])
    hh = _mm(xb, win_ref[:, 2 * D_MODEL:3 * D_MODEL])
    z = gc * hh
    z_ref[HALO:HALO + tm, :] = z
    z1 = z_ref[HALO - 1:HALO - 1 + tm, :]
    z2 = z_ref[HALO - 2:HALO - 2 + tm, :]
    zc = z2 * k_ref[0:1, :] + z1 * k_ref[1:2, :] + z * k_ref[2:3, :]
    mix = _mm((gb * zc).astype(BF16), wout_ref[...])
    out_ref[...] = _layer_norm(ALPHA * x_ref[...] + mix, g_ref[...], b_ref[...])
    st_ref[0] = z_ref[HALO + tm - (CONV_W - 1):HALO + tm, :]


def _sconv_prompt(x, win, k, wout, ln_g, ln_b, batch, seq, tm):
    m = x.shape[0]
    tiles_per_seq = seq // tm
    act = pl.BlockSpec((tm, D_MODEL), lambda i: (i, 0))
    return pl.pallas_call(
        functools.partial(_sconv_body, tm=tm, tiles_per_seq=tiles_per_seq),
        out_shape=(jax.ShapeDtypeStruct((m, D_MODEL), F32),
                   jax.ShapeDtypeStruct((batch, CONV_W - 1, D_MODEL), F32)),
        grid=(m // tm,),
        in_specs=[act, _const_spec(win.shape), _const_spec(k.shape), _const_spec(wout.shape),
                  _const_spec((1, D_MODEL)), _const_spec((1, D_MODEL))],
        out_specs=(act, pl.BlockSpec((1, CONV_W - 1, D_MODEL), lambda i: (i // tiles_per_seq, 0, 0))),
        scratch_shapes=[pltpu.VMEM((HALO + tm, D_MODEL), F32)],
        compiler_params=_cparams(("arbitrary",)),
        name="sconv_mixer",
    )(x, win, k, wout, ln_g, ln_b)


def _sconv_step_body(x_ref, st_ref, win_ref, k_ref, wout_ref, g_ref, b_ref, out_ref, nst_ref):
    xb = x_ref[...].astype(BF16)
    gb = _mm(xb, win_ref[:, 0:D_MODEL])
    gc = _mm(xb, win_ref[:, D_MODEL:2 * D_MODEL])
    hh = _mm(xb, win_ref[:, 2 * D_MODEL:3 * D_MODEL])
    z = gc * hh
    z2 = st_ref[:, 0:D_MODEL]
    z1 = st_ref[:, D_MODEL:2 * D_MODEL]
    zc = z2 * k_ref[0:1, :] + z1 * k_ref[1:2, :] + z * k_ref[2:3, :]
    mix = _mm((gb * zc).astype(BF16), wout_ref[...])
    out_ref[...] = _layer_norm(ALPHA * x_ref[...] + mix, g_ref[...], b_ref[...])
    nst_ref[:, 0:D_MODEL] = z1
    nst_ref[:, D_MODEL:2 * D_MODEL] = z


def _sconv_step(x, st, win, k, wout, ln_g, ln_b):
    nb = x.shape[0]
    st2 = st.reshape(nb, (CONV_W - 1) * D_MODEL)
    ins = (x, st2, win, k, wout, ln_g, ln_b)
    out, nst = pl.pallas_call(
        _sconv_step_body,
        out_shape=(jax.ShapeDtypeStruct(x.shape, F32), jax.ShapeDtypeStruct(st2.shape, F32)),
        grid=(1,),
        in_specs=[_const_spec(a.shape) for a in ins],
        out_specs=(_const_spec(x.shape), _const_spec(st2.shape)),
        compiler_params=_cparams(("arbitrary",)),
        name="sconv_mixer_step",
    )(*ins)
    return out, nst.reshape(st.shape)


def kernel(x_prompt, x_sample, cache_kv_w128, cache_kv_w512, cache_kv_w2048, state_pool, state_sconv, state_ffn_conv, attn_w_qkv, attn_w_o, pool_w_in, pool_w_grp, pool_scale, pool_w_out, sconv_w_in, sconv_k, sconv_w_out, ffn_w_up, ffn_k, ffn_w_down, ln_g, ln_b):
    batch, seq, _ = x_prompt.shape
    nb = x_sample.shape[0]
    caches = (cache_kv_w128, cache_kv_w512, cache_kv_w2048)

    w_qkv = attn_w_qkv.astype(BF16)
    w_o = attn_w_o.astype(BF16)
    p_in = pool_w_in.astype(BF16)
    p_grp = pool_w_grp.astype(BF16)
    p_out = pool_w_out.astype(BF16)
    s_in = sconv_w_in.astype(BF16)
    s_out = sconv_w_out.astype(BF16)
    f_up = ffn_w_up.astype(BF16)
    f_down = ffn_w_down.astype(BF16)

    tm_qkv = 1024
    tables = [_rope_tables(jnp.arange(seq).reshape(seq // tm_qkv, tm_qkv // dil, dil)
                           .transpose(0, 2, 1).reshape(seq)) for _, dil in ATTN_GROUPS]
    cos_p = [t[0] for t in tables]
    sin_p = [t[1] for t in tables]
    cos_s, sin_s = _rope_tables(jnp.full((nb,), PAST_LEN))

    xp = x_prompt.reshape(batch * seq, D_MODEL)
    xs = x_sample.reshape(nb, D_MODEL)

    caches_t = [c.transpose(0, 1, 3, 4, 5, 2).reshape(c.shape[0], c.shape[1], 2, D_MODEL, c.shape[2])
                for c in caches]
    n_attn = caches[0].shape[0]
    kv_p = [None for _ in ATTN_GROUPS]
    kv_s = [None for _ in ATTN_GROUPS]
    pool_p, pool_s, sconv_p, sconv_s, ffn_p, ffn_s = [], [], [], [], [], []
    ia = ib = ic = 0
    tm_ffn = 1024
    tm = 512
    for i in range(DEPTH):
        kind = i % N_MIXERS
        g1 = ln_g[i, 0].reshape(1, D_MODEL)
        b1 = ln_b[i, 0].reshape(1, D_MODEL)
        g2 = ln_g[i, 1].reshape(1, D_MODEL)
        b2 = ln_b[i, 1].reshape(1, D_MODEL)
        if kind == 0:
            x3 = xp.reshape(batch, seq, D_MODEL)
            outs_p, lses_p, outs_s, lses_s = [], [], [], []
            for g, (win, dil) in enumerate(ATTN_GROUPS):
                n = seq // dil
                qkv_g = _qkv_rope(x3, w_qkv, ia, g, dil, cos_p[g], sin_p[g], tm_qkv, BF16,
                                  q_scale=ATTN_SCALE * LOG2E)
                o, l = _attn_prompt(qkv_g.reshape(batch * dil, n, 3 * D_MODEL), g)
                outs_p.append(o.reshape(batch, dil, n, D_MODEL))
                lses_p.append(l.reshape(batch, dil, n, LANES))
                kv_p[g] = _kv_tail(x3, w_qkv, cos_p[0], sin_p[0], kv_p[g], ia, n_attn, g, min(win, seq))
                qkv_s = _qkv_rope(xs.reshape(1, nb, D_MODEL), w_qkv, ia, g, 1, cos_s, sin_s, nb, F32)
                kv_s[g], o, l = _cache_step(qkv_s.reshape(nb, 3 * D_MODEL), caches_t[g], kv_s[g], ia, g, dil)
                outs_s.append(o.reshape(nb, D_MODEL))
                lses_s.append(l.reshape(nb, LANES))
            xp = _merge_wo_ln(outs_p, lses_p, x3, w_o[ia], g1, b1, tm).reshape(batch * seq, D_MODEL)
            xs = _merge_step(outs_s, lses_s, xs, w_o[ia], g1, b1)
            ia += 1
        elif kind == 1:
            sc = pool_scale[ib].reshape(1, D_MODEL)
            xp, st = _pool_prompt(xp, p_in[ib], p_grp[ib], sc, p_out[ib], g1, b1, batch, seq, tm)
            pool_p.append(st)
            xs, st = _pool_step(xs, state_pool[ib], p_in[ib], p_grp[ib], sc, p_out[ib], g1, b1)
            pool_s.append(st)
            ib += 1
        else:
            xp, st = _sconv_prompt(xp, s_in[ic], sconv_k[ic], s_out[ic], g1, b1, batch, seq, tm)
            sconv_p.append(st)
            xs, st = _sconv_step(xs, state_sconv[ic], s_in[ic], sconv_k[ic], s_out[ic], g1, b1)
            sconv_s.append(st)
            ic += 1
        xp, st = _ffn_prompt(xp, f_up, ffn_k[i], f_down, i, g2, b2, batch, seq, tm_ffn)
        ffn_p.append(st)
        xs, st = _ffn_step(xs, state_ffn_conv[i], f_up, ffn_k[i], f_down, i, g2, b2)
        ffn_s.append(st)

    y_prompt = xp.reshape(batch, seq, D_MODEL)
    y_sample = xs.reshape(nb, 1, D_MODEL)

    def time_major(t, rows):
        t = t.reshape(n_attn, rows, 2, N_HEADS, HEAD_DIM, t.shape[-1])
        return t.transpose(0, 1, 5, 2, 3, 4)

    return (y_prompt, y_sample,
            time_major(kv_p[0], batch), time_major(kv_s[0], nb),
            time_major(kv_p[1], batch), time_major(kv_s[1], nb),
            time_major(kv_p[2], batch), time_major(kv_s[2], nb),
            jnp.stack(pool_p), jnp.stack(pool_s),
            jnp.stack(sconv_p), jnp.stack(sconv_s),
            jnp.stack(ffn_p), jnp.stack(ffn_s))
```

```python
import functools

import jax
import jax.numpy as jnp
from jax import lax
from jax.experimental import pallas as pl
from jax.experimental.pallas import tpu as pltpu

F32 = jnp.float32
BF16 = jnp.bfloat16

D_MODEL = 1024
DEPTH = 4
PAST_LEN = 8192
N_MIXERS = 3
ATTN_GROUPS = ((128, 1), (512, 4), (2048, 16))
N_GROUPS = len(ATTN_GROUPS)
HEAD_DIM = 64
N_HEADS = D_MODEL // HEAD_DIM
ATTN_SCALE = HEAD_DIM ** -0.5
ROPE_THETA = 10000.0
POOL_WINDOWS = (2, 4, 8, 16)
POOL_GROUP = D_MODEL // len(POOL_WINDOWS)
POOL_PAST = max(POOL_WINDOWS) - 1
CONV_W = 3
D_FF = ((8 * D_MODEL // 3 + 127) // 128) * 128
ALPHA = (2.0 * DEPTH) ** 0.25
LN_EPS = 1e-5
QKV_COLS = N_GROUPS * 3 * D_MODEL

LANES = 128
HALO = 16
ATTN_BLOCK = 128
FF_CHUNK = 256
CACHE_BLOCK_BYTES = 8 * 1024 * 1024
KV_TAIL_TILE = 512
QKV_ROW_CHUNK = 128
SPLIT_STRIDE = 4
ATTN_STEP_ROWS = 1024
VMEM_LIMIT = 56 * 1024 * 1024
NEG = -1e30
LOG2E = 1.4426950408889634
LN2 = 0.6931471805599453


def _cparams(sem):
    return pltpu.CompilerParams(dimension_semantics=sem, vmem_limit_bytes=VMEM_LIMIT)


def _layer_norm(z, g, b):
    mu = jnp.mean(z, axis=-1, keepdims=True)
    zc = z - mu
    var = jnp.mean(zc * zc, axis=-1, keepdims=True)
    return zc * lax.rsqrt(var + LN_EPS) * g + b


def _mm(a, w):
    return jnp.dot(a, w, preferred_element_type=F32)


def _gelu(x):
    return 0.5 * x * (1.0 + lax.erf(x * (2.0 ** -0.5)))


def _const_spec(shape):
    nd = len(shape)
    return pl.BlockSpec(shape, lambda *_: (0,) * nd)


def _layer_spec(shape, layer, resident=False):
    nd = len(shape)
    mode = pl.Buffered(1) if resident else None
    return pl.BlockSpec((None,) + tuple(shape), lambda *_: (layer,) + (0,) * nd, pipeline_mode=mode)


def _qkv_body(x_ref, w_ref, cos_ref, sin_ref, o_ref, xb_ref, slab_ref, *, chunk, q_scale, dil):
    tm = x_ref.shape[1]
    n = tm // dil
    piece = min(chunk, n)
    pre = SPLIT_STRIDE if dil > SPLIT_STRIDE else 1
    step = dil // pre
    if dil == 1:
        xb_ref[...] = x_ref[0].astype(BF16)
    else:
        for c in range(D_MODEL // LANES):
            slab_ref[0, c] = x_ref[0, :, c * LANES:(c + 1) * LANES]
        if pre > 1:
            for r1 in range(pre):
                for c in range(D_MODEL // LANES):
                    slab_ref[1, c, r1 * (tm // pre):(r1 + 1) * (tm // pre), :] = (
                        slab_ref[0, c, pl.ds(r1, tm // pre, stride=pre), :])
    src_slab = slab_ref.at[1 if pre > 1 else 0]

    def regroup(rc):
        for p in range(chunk // piece):
            off = rc * chunk + p * piece
            r, i0 = off // n, off % n
            start = (r % pre) * (tm // pre) + r // pre + i0 * step
            src = pl.ds(start, piece, stride=step)
            for c in range(D_MODEL // LANES):
                xb_ref[off:off + piece, c * LANES:(c + 1) * LANES] = src_slab[c, src, :].astype(BF16)

    lane = lax.broadcasted_iota(jnp.int32, (chunk, LANES), 1)
    first_half = (lane & (HEAD_DIM // 2)) == 0

    def store(rc, col, val):
        for p in range(chunk // piece):
            off = rc * chunk + p * piece
            o_ref[0, off // n, off % n:off % n + piece, col:col + val.shape[1]] = (
                val[p * piece:(p + 1) * piece].astype(o_ref.dtype))

    for unit in range(3):
        for rc in range(tm // chunk):
            rs = slice(rc * chunk, (rc + 1) * chunk)
            if unit == 0 and dil > 1:
                regroup(rc)
            y = _mm(xb_ref[rs, :], w_ref[:, unit * D_MODEL:(unit + 1) * D_MODEL])
            if unit == 2:
                store(rc, unit * D_MODEL, y)
                continue
            cos = cos_ref[rs, :]
            sin = sin_ref[rs, :]
            if unit == 0 and q_scale != 1.0:
                cos = cos * q_scale
                sin = sin * q_scale
            for c in range(D_MODEL // LANES):
                yc = y[:, c * LANES:(c + 1) * LANES]
                partner = jnp.where(first_half,
                                    pltpu.roll(yc, LANES - HEAD_DIM // 2, 1),
                                    pltpu.roll(yc, HEAD_DIM // 2, 1))
                store(rc, unit * D_MODEL + c * LANES, yc * cos + partner * sin)


def _qkv_rope(x3, w, layer, g, dil, cos, sin, tm, out_dtype, q_scale=1.0):
    nseq, seq, _ = x3.shape
    table = pl.BlockSpec((tm, LANES), lambda b, i: (i, 0))
    return pl.pallas_call(
        functools.partial(_qkv_body, chunk=min(tm, QKV_ROW_CHUNK), q_scale=q_scale, dil=dil),
        out_shape=jax.ShapeDtypeStruct((nseq, dil, seq // dil, 3 * D_MODEL), out_dtype),
        grid=(nseq, seq // tm),
        in_specs=[
            pl.BlockSpec((1, tm, D_MODEL), lambda b, i: (b, i, 0)),
            pl.BlockSpec((None, D_MODEL, 3 * D_MODEL), lambda b, i: (layer, 0, g)),
            table, table,
        ],
        out_specs=pl.BlockSpec((1, dil, tm // dil, 3 * D_MODEL), lambda b, i: (b, 0, i, 0)),
        scratch_shapes=[pltpu.VMEM((tm, D_MODEL), BF16),
                        pltpu.VMEM((2 if dil > SPLIT_STRIDE else 1, D_MODEL // LANES,
                                    tm if dil > 1 else 8, LANES), F32)],
        compiler_params=_cparams(("arbitrary", "arbitrary")),
        name=f"qkv_rope_g{g}",
    )(x3, w, cos, sin)


def _store_classes(slab_ref, src_ref, dil):
    n = src_ref.shape[2]
    for r in range(dil):
        for c in range(src_ref.shape[3] // LANES):
            slab_ref[c, pl.ds(r, n, stride=dil), :] = src_ref[0, r, :, c * LANES:(c + 1) * LANES]


def _rope_tables(pos):
    half = HEAD_DIM // 2
    inv = ROPE_THETA ** (-jnp.arange(half, dtype=F32) / half)
    ang = pos.astype(F32)[:, None] * inv[None, :]
    cos = jnp.cos(ang)
    sin = jnp.sin(ang)
    reps = LANES // HEAD_DIM
    cos_t = jnp.tile(jnp.concatenate([cos, cos], axis=1), (1, reps))
    sin_t = jnp.tile(jnp.concatenate([-sin, sin], axis=1), (1, reps))
    return cos_t, sin_t


def _band_mask(has_prev):
    tq = ATTN_BLOCK
    row = lax.broadcasted_iota(jnp.int32, (tq, 2 * tq), 0)
    col = lax.broadcasted_iota(jnp.int32, (tq, 2 * tq), 1)
    delta = row + tq - col
    return (delta >= 0) & (delta <= tq) & ((col >= tq) | has_prev)


def _attn_pair(qp, k2, v2, valid, low):
    outs, lses = [], []
    for half in range(2):
        sel = low if half == 0 else jnp.logical_not(low)
        qm = jnp.where(sel, qp, 0.0)
        s = lax.dot_general(qm, k2, (((1,), (1,)), ((), ())), preferred_element_type=F32)
        s = jnp.where(valid, s, NEG)
        m = jnp.max(s, axis=1, keepdims=True)
        p = jnp.exp2(s - m)
        l = jnp.sum(p, axis=1, keepdims=True)
        pv = _mm(p.astype(BF16), v2)
        outs.append(pv * (1.0 / l))
        lses.append((m + jnp.log2(l)) * LN2)
    return jnp.where(low, outs[0], outs[1]), lses[0], lses[1]


def _attn_body(q_ref, kp_ref, kc_ref, vp_ref, vc_ref, o_ref, lse_ref, k_ref, v_ref):
    rows = q_ref.shape[1]
    not_first = pl.program_id(1) > 0
    k_ref[0:ATTN_BLOCK, :] = kp_ref[0]
    k_ref[ATTN_BLOCK:ATTN_BLOCK + rows, :] = kc_ref[0]
    v_ref[0:ATTN_BLOCK, :] = vp_ref[0]
    v_ref[ATTN_BLOCK:ATTN_BLOCK + rows, :] = vc_ref[0]
    lane = lax.broadcasted_iota(jnp.int32, (ATTN_BLOCK, LANES), 1)
    low = lane < HEAD_DIM
    for sb in range(rows // ATTN_BLOCK):
        valid = _band_mask(True if sb > 0 else not_first)
        qs = slice(sb * ATTN_BLOCK, (sb + 1) * ATTN_BLOCK)
        ks = slice(sb * ATTN_BLOCK, (sb + 2) * ATTN_BLOCK)
        lse_full = jnp.zeros((ATTN_BLOCK, LANES), F32)
        for hp in range(D_MODEL // LANES):
            cs = slice(hp * LANES, (hp + 1) * LANES)
            o, lse0, lse1 = _attn_pair(q_ref[0, qs, cs], k_ref[ks, cs], v_ref[ks, cs], valid, low)
            o_ref[0, qs, cs] = o
            lse_full = jnp.where(lane == 2 * hp, lse0, lse_full)
            lse_full = jnp.where(lane == 2 * hp + 1, lse1, lse_full)
        lse_ref[0, qs, :] = lse_full


def _attn_prompt(qkv3, g):
    nseq, n, _ = qkv3.shape
    rows = min(n, ATTN_STEP_ROWS)
    per_step = rows // ATTN_BLOCK
    blk = (1, rows, D_MODEL)

    def cur(unit):
        return pl.BlockSpec(blk, lambda b, i: (b, i, unit))

    def prev(unit):
        return pl.BlockSpec((1, ATTN_BLOCK, D_MODEL),
                            lambda b, i: (b, jnp.maximum(i * per_step - 1, 0), unit))

    return pl.pallas_call(
        _attn_body,
        out_shape=(jax.ShapeDtypeStruct((nseq, n, D_MODEL), F32),
                   jax.ShapeDtypeStruct((nseq, n, LANES), F32)),
        grid=(nseq, n // rows),
        in_specs=[cur(0), prev(1), cur(1), prev(2), cur(2)],
        out_specs=(pl.BlockSpec(blk, lambda b, i: (b, i, 0)),
                   pl.BlockSpec((1, rows, LANES), lambda b, i: (b, i, 0))),
        scratch_shapes=[pltpu.VMEM((ATTN_BLOCK + rows, D_MODEL), BF16),
                        pltpu.VMEM((ATTN_BLOCK + rows, D_MODEL), BF16)],
        compiler_params=_cparams(("arbitrary", "arbitrary")),
        name=f"attn_prompt_g{g}",
    )(qkv3, qkv3, qkv3, qkv3, qkv3)


def _cache_step_body(*refs, heads, win, dil, aliased):
    q_ref, knr_ref, vnr_ref, knt_ref, vnt_ref, c_ref = refs[:6]
    nc_ref, o_ref, lse_ref = refs[-3:]
    b = pl.program_id(0)
    hc = pl.program_id(1)
    rows = heads * HEAD_DIM
    sub = 16
    rowh = lax.broadcasted_iota(jnp.int32, (sub, rows), 0)
    own = rowh == lax.broadcasted_iota(jnp.int32, (sub, rows), 1) // HEAD_DIM
    q = q_ref[0] * ATTN_SCALE
    qbd = jnp.where(own, jnp.broadcast_to(q, (sub, rows)), 0.0).astype(BF16)
    k = c_ref[0]
    v = c_ref[1]
    t = lax.broadcasted_iota(jnp.int32, (sub, win), 1)
    s = jnp.where((t & (dil - 1)) == 0, _mm(qbd, k.astype(BF16)), NEG)
    knr = knr_ref[0].astype(BF16).astype(F32)
    vnr = vnr_ref[0].astype(BF16).astype(F32)
    s_new = jnp.sum(qbd.astype(F32) * knr, axis=1, keepdims=True)
    m = jnp.maximum(jnp.max(s, axis=1, keepdims=True), s_new)
    p = jnp.exp(s - m)
    p_new = jnp.exp(s_new - m)
    l = jnp.sum(p, axis=1, keepdims=True) + p_new
    pv = lax.dot_general(p.astype(BF16), v.astype(BF16), (((1,), (1,)), ((), ())),
                         preferred_element_type=F32)
    o16 = (pv + p_new.astype(BF16).astype(F32) * vnr) * (1.0 / l)
    o_ref[0] = jnp.sum(jnp.where(own, o16, 0.0), axis=0, keepdims=True)
    lse = m + jnp.log(l)
    r2 = lax.broadcasted_iota(jnp.int32, (sub, LANES), 0)
    l2 = lax.broadcasted_iota(jnp.int32, (sub, LANES), 1)
    lse_row = jnp.sum(jnp.where((r2 < heads) & (r2 + hc * heads == l2), lse, 0.0), axis=0, keepdims=True)

    @pl.when(hc == 0)
    def _():
        lse_ref[0] = lse_row

    @pl.when(hc > 0)
    def _():
        lse_ref[0] = lse_ref[0] + lse_row

    pick = lax.broadcasted_iota(jnp.int32, knt_ref.shape, 1) == b
    kn_col = jnp.sum(jnp.where(pick, knt_ref[...], 0.0), axis=1, keepdims=True)
    vn_col = jnp.sum(jnp.where(pick, vnt_ref[...], 0.0), axis=1, keepdims=True)
    last = lax.broadcasted_iota(jnp.int32, (rows, win), 1) == win - 1
    nc_ref[0] = jnp.where(last, kn_col, pltpu.roll(k, win - 1, 1))
    nc_ref[1] = jnp.where(last, vn_col, pltpu.roll(v, win - 1, 1))


def _cache_step(qkv_s, cache_t, prev_out, layer, g, dil):
    n_layers, nb, _, _, win = cache_t.shape
    heads = max(1, min(N_HEADS, CACHE_BLOCK_BYTES // (2 * HEAD_DIM * win * 4)))
    rows = heads * HEAD_DIM
    per_unit = D_MODEL // rows
    aliased = prev_out is not None
    qkv_s3 = qkv_s.reshape(nb, 1, 3 * D_MODEL)
    qkv_t = qkv_s.T

    def row(u):
        return pl.BlockSpec((1, 1, rows), lambda b, hc: (b, 0, u * per_unit + hc))

    def col(u):
        return pl.BlockSpec((rows, nb), lambda b, hc: (u * per_unit + hc, 0))

    cblk = (None, None, 2, rows, win)
    in_specs = [row(0), row(1), row(2), col(1), col(2),
                pl.BlockSpec(cblk, lambda b, hc: (layer, b, 0, hc, 0))]
    args = [qkv_s3, qkv_s3, qkv_s3, qkv_t, qkv_t, cache_t]
    if aliased:
        in_specs.append(pl.BlockSpec(memory_space=pl.ANY))
        args.append(prev_out)
    return pl.pallas_call(
        functools.partial(_cache_step_body, heads=heads, win=win, dil=dil, aliased=aliased),
        out_shape=(jax.ShapeDtypeStruct(cache_t.shape, F32),
                   jax.ShapeDtypeStruct((nb, 1, D_MODEL), F32),
                   jax.ShapeDtypeStruct((nb, 1, LANES), F32)),
        grid=(nb, N_HEADS // heads),
        in_specs=in_specs,
        out_specs=(pl.BlockSpec(cblk, lambda b, hc: (layer, b, 0, hc, 0)),
                   pl.BlockSpec((1, 1, rows), lambda b, hc: (b, 0, hc)),
                   pl.BlockSpec((1, 1, LANES), lambda b, hc: (b, 0, 0))),
        input_output_aliases={6: 0} if aliased else {},
        compiler_params=_cparams(("arbitrary", "arbitrary")),
        name=f"cache_step_g{g}",
    )(*args)


def _kv_tail_body(*refs):
    x_ref, wk_ref, wv_ref, cos_ref, sin_ref = refs[:5]
    o_ref = refs[-1]
    chunk = LANES
    lane = lax.broadcasted_iota(jnp.int32, (chunk, LANES), 1)
    first_half = (lane & (HEAD_DIM // 2)) == 0
    xb = x_ref[0].astype(BF16)
    for rc in range(x_ref.shape[1] // chunk):
        rs = slice(rc * chunk, (rc + 1) * chunk)
        cos = cos_ref[rs, :]
        sin = sin_ref[rs, :]
        yk = _mm(xb[rs], wk_ref[...])
        yv = _mm(xb[rs], wv_ref[...])
        for c in range(D_MODEL // LANES):
            cs = slice(c * LANES, (c + 1) * LANES)
            yc = yk[:, cs]
            partner = jnp.where(first_half,
                                pltpu.roll(yc, LANES - HEAD_DIM // 2, 1),
                                pltpu.roll(yc, HEAD_DIM // 2, 1))
            o_ref[0, cs, rs] = (yc * cos + partner * sin).T
            o_ref[1, cs, rs] = yv[:, cs].T


def _kv_tail(x3, w, cos, sin, prev_out, layer, n_layers, g, win):
    batch, seq, _ = x3.shape
    tt = min(win, KV_TAIL_TILE)
    first = (seq - win) // tt
    aliased = prev_out is not None
    table = pl.BlockSpec((tt, LANES), lambda b, i: (first + i, 0))
    in_specs = [pl.BlockSpec((1, tt, D_MODEL), lambda b, i: (b, first + i, 0)),
                pl.BlockSpec((None, D_MODEL, D_MODEL), lambda b, i: (layer, 0, 3 * g + 1)),
                pl.BlockSpec((None, D_MODEL, D_MODEL), lambda b, i: (layer, 0, 3 * g + 2)),
                table, table]
    args = [x3, w, w, cos, sin]
    if aliased:
        in_specs.append(pl.BlockSpec(memory_space=pl.ANY))
        args.append(prev_out)
    return pl.pallas_call(
        _kv_tail_body,
        out_shape=jax.ShapeDtypeStruct((n_layers, batch, 2, D_MODEL, win), F32),
        grid=(batch, win // tt),
        in_specs=in_specs,
        out_specs=pl.BlockSpec((None, None, 2, D_MODEL, tt), lambda b, i: (layer, b, 0, 0, i)),
        input_output_aliases={5: 0} if aliased else {},
        compiler_params=_cparams(("arbitrary", "arbitrary")),
        name=f"kv_tail_g{g}",
    )(*args)


def _group_weights(ls, expand):
    m = jnp.maximum(jnp.maximum(ls[0], ls[1]), ls[2])
    ex = [jnp.exp(l - m) for l in ls]
    inv_den = 1.0 / (ex[0] + ex[1] + ex[2])
    wfs = []
    for ex_g in ex:
        w = ex_g * inv_den
        w_hi = w.astype(BF16)
        w_lo = (w - w_hi.astype(F32)).astype(BF16)
        wfs.append(_mm(w_hi, expand) + _mm(w_lo, expand))
    return wfs


def _merge_step_body(o0_ref, o1_ref, o2_ref, l0_ref, l1_ref, l2_ref, x_ref, wo_ref, e_ref,
                     g_ref, b_ref, out_ref):
    wfs = _group_weights([l0_ref[...], l1_ref[...], l2_ref[...]], e_ref[...])
    merged = wfs[0] * o0_ref[...] + wfs[1] * o1_ref[...] + wfs[2] * o2_ref[...]
    mix = _mm(merged.astype(BF16), wo_ref[...])
    out_ref[...] = _layer_norm(ALPHA * x_ref[...] + mix, g_ref[...], b_ref[...])


def _head_expand():
    head = jnp.arange(LANES)[:, None]
    lane_head = jnp.arange(D_MODEL)[None, :] // HEAD_DIM
    return (head == lane_head).astype(BF16)


def _merge_step(outs, lses, x, wo, ln_g, ln_b):
    ins = (*outs, *lses, x, wo, _head_expand(), ln_g, ln_b)
    return pl.pallas_call(
        _merge_step_body,
        out_shape=jax.ShapeDtypeStruct(x.shape, F32),
        grid=(1,),
        in_specs=[_const_spec(a.shape) for a in ins],
        out_specs=_const_spec(x.shape),
        compiler_params=_cparams(("arbitrary",)),
        name="merge_wo_ln_step",
    )(*ins)


def _merge_body(o0_ref, o1_ref, o2_ref, l0_ref, l1_ref, l2_ref, x_ref, wo_ref, e_ref,
                g_ref, b_ref, out_ref, oslab_ref, lslab_ref, merged_ref):
    o_refs = (o0_ref, o1_ref, o2_ref)
    l_refs = (l0_ref, l1_ref, l2_ref)
    for gi, (_, dil) in enumerate(ATTN_GROUPS):
        _store_classes(oslab_ref.at[gi], o_refs[gi], dil)
        _store_classes(lslab_ref.at[gi], l_refs[gi], dil)
    wfs = _group_weights([lslab_ref[gi, 0] for gi in range(N_GROUPS)], e_ref[...])
    for c in range(D_MODEL // LANES):
        cs = slice(c * LANES, (c + 1) * LANES)
        t = wfs[0][:, cs] * oslab_ref[0, c] + wfs[1][:, cs] * oslab_ref[1, c] + wfs[2][:, cs] * oslab_ref[2, c]
        merged_ref[:, cs] = t.astype(BF16)
    mix = _mm(merged_ref[...], wo_ref[...])
    out_ref[0] = _layer_norm(ALPHA * x_ref[0] + mix, g_ref[...], b_ref[...])


def _merge_wo_ln(outs, lses, x3, wo, ln_g, ln_b, tm):
    batch, seq, _ = x3.shape
    expand = _head_expand()
    act = pl.BlockSpec((1, tm, D_MODEL), lambda b, i: (b, i, 0))

    def classes(dil, width):
        return pl.BlockSpec((1, dil, tm // dil, width), lambda b, i: (b, 0, i, 0))

    pairs = D_MODEL // LANES
    return pl.pallas_call(
        _merge_body,
        out_shape=jax.ShapeDtypeStruct((batch, seq, D_MODEL), F32),
        grid=(batch, seq // tm),
        in_specs=[classes(dil, D_MODEL) for _, dil in ATTN_GROUPS]
        + [classes(dil, LANES) for _, dil in ATTN_GROUPS]
        + [act, _const_spec((D_MODEL, D_MODEL)), _const_spec((LANES, D_MODEL)),
           _const_spec((1, D_MODEL)), _const_spec((1, D_MODEL))],
        out_specs=act,
        scratch_shapes=[pltpu.VMEM((N_GROUPS, pairs, tm, LANES), F32),
                        pltpu.VMEM((N_GROUPS, 1, tm, LANES), F32),
                        pltpu.VMEM((tm, D_MODEL), BF16)],
        compiler_params=_cparams(("arbitrary", "arbitrary")),
        name="merge_wo_ln",
    )(*outs, *lses, x3, wo, expand, ln_g, ln_b)


def _load_halo(ext_ref, tm, tiles_per_seq):
    first = (pl.program_id(0) % tiles_per_seq) == 0

    @pl.when(first)
    def _():
        ext_ref[0:HALO, :] = jnp.zeros((HALO, ext_ref.shape[1]), F32)

    @pl.when(jnp.logical_not(first))
    def _():
        ext_ref[0:HALO, :] = ext_ref[tm:tm + HALO, :]


def _ffn_body(x_ref, wu_ref, k_ref, wd_ref, g_ref, b_ref, out_ref, st_ref,
              xb_ref, a_ref, carry_ref, acc_ref, *, tm, tiles_per_seq):
    @pl.when((pl.program_id(0) % tiles_per_seq) == 0)
    def _():
        carry_ref[...] = jnp.zeros(carry_ref.shape, F32)

    xb_ref[...] = x_ref[...].astype(BF16)

    def up(c):
        a = _mm(xb_ref[...], wu_ref[:, c * FF_CHUNK:(c + 1) * FF_CHUNK])
        gate = _mm(xb_ref[...], wu_ref[:, D_FF + c * FF_CHUNK:D_FF + (c + 1) * FF_CHUNK])
        return a, gate

    n_chunks = D_FF // FF_CHUNK
    nxt = up(0)
    for c in range(n_chunks):
        cs = slice(c * FF_CHUNK, (c + 1) * FF_CHUNK)
        buf = a_ref.at[c % 2]
        a, gate = nxt
        if c + 1 < n_chunks:
            nxt = up(c + 1)
        buf[0:HALO, :] = carry_ref[:, cs]
        buf[HALO:HALO + tm, :] = a
        a1 = buf[HALO - 1:HALO - 1 + tm, :]
        a2 = buf[HALO - 2:HALO - 2 + tm, :]
        carry_ref[:, cs] = buf[tm:tm + HALO, :]
        ac = a2 * k_ref[0:1, cs] + a1 * k_ref[1:2, cs] + a * k_ref[2:3, cs]
        hid = (_gelu(ac) * gate).astype(BF16)
        d = _mm(hid, wd_ref[cs, :])
        if c == 0:
            acc_ref[...] = d
        else:
            acc_ref[...] += d
    out_ref[...] = _layer_norm(ALPHA * x_ref[...] + acc_ref[...], g_ref[...], b_ref[...])
    st_ref[0] = carry_ref[HALO - (CONV_W - 1):HALO, :]


def _ffn_prompt(x, wu, k, wd, layer, ln_g, ln_b, batch, seq, tm):
    m = x.shape[0]
    tiles_per_seq = seq // tm
    act = pl.BlockSpec((tm, D_MODEL), lambda i: (i, 0))
    return pl.pallas_call(
        functools.partial(_ffn_body, tm=tm, tiles_per_seq=tiles_per_seq),
        out_shape=(jax.ShapeDtypeStruct((m, D_MODEL), F32),
                   jax.ShapeDtypeStruct((batch, CONV_W - 1, D_FF), F32)),
        grid=(m // tm,),
        in_specs=[act, _layer_spec((D_MODEL, 2 * D_FF), layer, resident=True), _const_spec((CONV_W, D_FF)),
                  _layer_spec((D_FF, D_MODEL), layer, resident=True),
                  _const_spec((1, D_MODEL)), _const_spec((1, D_MODEL))],
        out_specs=(act, pl.BlockSpec((1, CONV_W - 1, D_FF), lambda i: (i // tiles_per_seq, 0, 0))),
        scratch_shapes=[pltpu.VMEM((tm, D_MODEL), BF16),
                        pltpu.VMEM((2, HALO + tm, FF_CHUNK), F32),
                        pltpu.VMEM((HALO, D_FF), F32),
                        pltpu.VMEM((tm, D_MODEL), F32)],
        compiler_params=_cparams(("arbitrary",)),
        name="conv_ffn",
    )(x, wu, k, wd, ln_g, ln_b)


def _ffn_step_body(x_ref, st_ref, wu_ref, k_ref, wd_ref, g_ref, b_ref, out_ref, nst_ref):
    xb = x_ref[...].astype(BF16)
    a = _mm(xb, wu_ref[:, 0:D_FF])
    gate = _mm(xb, wu_ref[:, D_FF:2 * D_FF])
    a2 = st_ref[:, 0:D_FF]
    a1 = st_ref[:, D_FF:2 * D_FF]
    ac = a2 * k_ref[0:1, :] + a1 * k_ref[1:2, :] + a * k_ref[2:3, :]
    hid = (_gelu(ac) * gate).astype(BF16)
    f = _mm(hid, wd_ref[...])
    out_ref[...] = _layer_norm(ALPHA * x_ref[...] + f, g_ref[...], b_ref[...])
    nst_ref[:, 0:D_FF] = a1
    nst_ref[:, D_FF:2 * D_FF] = a


def _ffn_step(x, st, wu, k, wd, layer, ln_g, ln_b):
    nb = x.shape[0]
    st2 = st.reshape(nb, (CONV_W - 1) * D_FF)
    out, nst = pl.pallas_call(
        _ffn_step_body,
        out_shape=(jax.ShapeDtypeStruct(x.shape, F32), jax.ShapeDtypeStruct(st2.shape, F32)),
        grid=(1,),
        in_specs=[_const_spec(x.shape), _const_spec(st2.shape), _layer_spec(wu.shape[1:], layer),
                  _const_spec(k.shape), _layer_spec(wd.shape[1:], layer),
                  _const_spec(ln_g.shape), _const_spec(ln_b.shape)],
        out_specs=(_const_spec(x.shape), _const_spec(st2.shape)),
        compiler_params=_cparams(("arbitrary",)),
        name="conv_ffn_step",
    )(x, st2, wu, k, wd, ln_g, ln_b)
    return out, nst.reshape(st.shape)


def _pool_body(x_ref, win_ref, wg_ref, sc_ref, wout_ref, g_ref, b_ref, out_ref, st_ref,
               u_ref, *, tm, tiles_per_seq):
    _load_halo(u_ref, tm, tiles_per_seq)
    xb = x_ref[...].astype(BF16)
    u_ref[HALO:HALO + tm, :] = _mm(xb, win_ref[...])
    pos = (pl.program_id(0) % tiles_per_seq) * tm + lax.broadcasted_iota(jnp.int32, (tm, 1), 0)
    zs = []
    for gi, w in enumerate(POOL_WINDOWS):
        cs = slice(gi * POOL_GROUP, (gi + 1) * POOL_GROUP)
        u = u_ref[HALO:HALO + tm, cs]
        win_sum = u
        for j in range(1, w):
            win_sum = win_sum + u_ref[HALO - j:HALO - j + tm, cs]
        cnt = jnp.minimum(w, pos + 1).astype(F32)
        pooled = win_sum * (1.0 / cnt) - u
        z = _mm(pooled.astype(BF16), wg_ref[gi])
        zs.append(z * sc_ref[:, cs])
    zc = jnp.concatenate(zs, axis=1).astype(BF16)
    mix = _mm(zc, wout_ref[...])
    out_ref[...] = _layer_norm(ALPHA * x_ref[...] + mix, g_ref[...], b_ref[...])
    st_ref[0] = u_ref[HALO + tm - POOL_PAST:HALO + tm, :]


def _pool_prompt(x, win, wg, sc, wout, ln_g, ln_b, batch, seq, tm):
    m = x.shape[0]
    tiles_per_seq = seq // tm
    act = pl.BlockSpec((tm, D_MODEL), lambda i: (i, 0))
    return pl.pallas_call(
        functools.partial(_pool_body, tm=tm, tiles_per_seq=tiles_per_seq),
        out_shape=(jax.ShapeDtypeStruct((m, D_MODEL), F32),
                   jax.ShapeDtypeStruct((batch, POOL_PAST, D_MODEL), F32)),
        grid=(m // tm,),
        in_specs=[act, _const_spec(win.shape), _const_spec(wg.shape), _const_spec(sc.shape),
                  _const_spec(wout.shape), _const_spec((1, D_MODEL)), _const_spec((1, D_MODEL))],
        out_specs=(act, pl.BlockSpec((1, POOL_PAST, D_MODEL), lambda i: (i // tiles_per_seq, 0, 0))),
        scratch_shapes=[pltpu.VMEM((HALO + tm, D_MODEL), F32)],
        compiler_params=_cparams(("arbitrary",)),
        name="pool_mixer",
    )(x, win, wg, sc, wout, ln_g, ln_b)


def _pool_step_body(x_ref, st_ref, win_ref, wg_ref, sc_ref, wout_ref, g_ref, b_ref,
                    out_ref, nst_ref):
    xb = x_ref[...].astype(BF16)
    u_all = _mm(xb, win_ref[...])
    zs = []
    for gi, w in enumerate(POOL_WINDOWS):
        u = u_all[:, gi * POOL_GROUP:(gi + 1) * POOL_GROUP]
        win_sum = u
        for j in range(1, w):
            base = (POOL_PAST - j) * D_MODEL + gi * POOL_GROUP
            win_sum = win_sum + st_ref[:, base:base + POOL_GROUP]
        cnt = float(min(w, PAST_LEN + 1))
        pooled = win_sum * (1.0 / cnt) - u
        z = _mm(pooled.astype(BF16), wg_ref[gi])
        zs.append(z * sc_ref[:, gi * POOL_GROUP:(gi + 1) * POOL_GROUP])
    zc = jnp.concatenate(zs, axis=1).astype(BF16)
    mix = _mm(zc, wout_ref[...])
    out_ref[...] = _layer_norm(ALPHA * x_ref[...] + mix, g_ref[...], b_ref[...])
    keep = (POOL_PAST - 1) * D_MODEL
    nst_ref[:, 0:keep] = st_ref[:, D_MODEL:POOL_PAST * D_MODEL]
    nst_ref[:, keep:keep + D_MODEL] = u_all


def _pool_step(x, st, win, wg, sc, wout, ln_g, ln_b):
    nb = x.shape[0]
    st2 = st.reshape(nb, POOL_PAST * D_MODEL)
    ins = (x, st2, win, wg, sc, wout, ln_g, ln_b)
    out, nst = pl.pallas_call(
        _pool_step_body,
        out_shape=(jax.ShapeDtypeStruct(x.shape, F32), jax.ShapeDtypeStruct(st2.shape, F32)),
        grid=(1,),
        in_specs=[_const_spec(a.shape) for a in ins],
        out_specs=(_const_spec(x.shape), _const_spec(st2.shape)),
        compiler_params=_cparams(("arbitrary",)),
        name="pool_mixer_step",
    )(*ins)
    return out, nst.reshape(st.shape)


def _sconv_body(x_ref, win_ref, k_ref, wout_ref, g_ref, b_ref, out_ref, st_ref,
                z_ref, *, tm, tiles_per_seq):
    _load_halo(z_ref, tm, tiles_per_seq)
    xb = x_ref[...].astype(BF16)
    gb = _mm(xb, win_ref[:, 0:D_MODEL])
    gc = _mm(xb, win_ref[:, D_MODEL:2 * D_MODEL])
    hh = _mm(xb, win_ref[:, 2 * D_MODEL:3 * D_MODEL])
    z = gc * hh
    z_ref[HALO:HALO + tm, :] = z
    z1 = z_ref[HALO - 1:HALO - 1 + tm, :]
    z2 = z_ref[HALO - 2:HALO - 2 + tm, :]
    zc = z2 * k_ref[0:1, :] + z1 * k_ref[1:2, :] + z * k_ref[2:3, :]
    mix = _mm((gb * zc).astype(BF16), wout_ref[...])
    out_ref[...] = _layer_norm(ALPHA * x_ref[...] + mix, g_ref[...], b_ref[...])
    st_ref[0] = z_ref[HALO + tm - (CONV_W - 1):HALO + tm, :]


def _sconv_prompt(x, win, k, wout, ln_g, ln_b, batch, seq, tm):
    m = x.shape[0]
    tiles_per_seq = seq // tm
    act = pl.BlockSpec((tm, D_MODEL), lambda i: (i, 0))
    return pl.pallas_call(
        functools.partial(_sconv_body, tm=tm, tiles_per_seq=tiles_per_seq),
        out_shape=(jax.ShapeDtypeStruct((m, D_MODEL), F32),
                   jax.ShapeDtypeStruct((batch, CONV_W - 1, D_MODEL), F32)),
        grid=(m // tm,),
        in_specs=[act, _const_spec(win.shape), _const_spec(k.shape), _const_spec(wout.shape),
                  _const_spec((1, D_MODEL)), _const_spec((1, D_MODEL))],
        out_specs=(act, pl.BlockSpec((1, CONV_W - 1, D_MODEL), lambda i: (i // tiles_per_seq, 0, 0))),
        scratch_shapes=[pltpu.VMEM((HALO + tm, D_MODEL), F32)],
        compiler_params=_cparams(("arbitrary",)),
        name="sconv_mixer",
    )(x, win, k, wout, ln_g, ln_b)


def _sconv_step_body(x_ref, st_ref, win_ref, k_ref, wout_ref, g_ref, b_ref, out_ref, nst_ref):
    xb = x_ref[...].astype(BF16)
    gb = _mm(xb, win_ref[:, 0:D_MODEL])
    gc = _mm(xb, win_ref[:, D_MODEL:2 * D_MODEL])
    hh = _mm(xb, win_ref[:, 2 * D_MODEL:3 * D_MODEL])
    z = gc * hh
    z2 = st_ref[:, 0:D_MODEL]
    z1 = st_ref[:, D_MODEL:2 * D_MODEL]
    zc = z2 * k_ref[0:1, :] + z1 * k_ref[1:2, :] + z * k_ref[2:3, :]
    mix = _mm((gb * zc).astype(BF16), wout_ref[...])
    out_ref[...] = _layer_norm(ALPHA * x_ref[...] + mix, g_ref[...], b_ref[...])
    nst_ref[:, 0:D_MODEL] = z1
    nst_ref[:, D_MODEL:2 * D_MODEL] = z


def _sconv_step(x, st, win, k, wout, ln_g, ln_b):
    nb = x.shape[0]
    st2 = st.reshape(nb, (CONV_W - 1) * D_MODEL)
    ins = (x, st2, win, k, wout, ln_g, ln_b)
    out, nst = pl.pallas_call(
        _sconv_step_body,
        out_shape=(jax.ShapeDtypeStruct(x.shape, F32), jax.ShapeDtypeStruct(st2.shape, F32)),
        grid=(1,),
        in_specs=[_const_spec(a.shape) for a in ins],
        out_specs=(_const_spec(x.shape), _const_spec(st2.shape)),
        compiler_params=_cparams(("arbitrary",)),
        name="sconv_mixer_step",
    )(*ins)
    return out, nst.reshape(st.shape)


def kernel(x_prompt, x_sample, cache_kv_w128, cache_kv_w512, cache_kv_w2048, state_pool, state_sconv, state_ffn_conv, attn_w_qkv, attn_w_o, pool_w_in, pool_w_grp, pool_scale, pool_w_out, sconv_w_in, sconv_k, sconv_w_out, ffn_w_up, ffn_k, ffn_w_down, ln_g, ln_b):
    batch, seq, _ = x_prompt.shape
    nb = x_sample.shape[0]
    caches = (cache_kv_w128, cache_kv_w512, cache_kv_w2048)

    w_qkv = attn_w_qkv.astype(BF16)
    w_o = attn_w_o.astype(BF16)
    p_in = pool_w_in.astype(BF16)
    p_grp = pool_w_grp.astype(BF16)
    p_out = pool_w_out.astype(BF16)
    s_in = sconv_w_in.astype(BF16)
    s_out = sconv_w_out.astype(BF16)
    f_up = ffn_w_up.astype(BF16)
    f_down = ffn_w_down.astype(BF16)

    tm_qkv = 1024
    tables = [_rope_tables(jnp.arange(seq).reshape(seq // tm_qkv, tm_qkv // dil, dil)
                           .transpose(0, 2, 1).reshape(seq)) for _, dil in ATTN_GROUPS]
    cos_p = [t[0] for t in tables]
    sin_p = [t[1] for t in tables]
    cos_s, sin_s = _rope_tables(jnp.full((nb,), PAST_LEN))

    xp = x_prompt.reshape(batch * seq, D_MODEL)
    xs = x_sample.reshape(nb, D_MODEL)

    caches_t = [c.transpose(0, 1, 3, 4, 5, 2).reshape(c.shape[0], c.shape[1], 2, D_MODEL, c.shape[2])
                for c in caches]
    n_attn = caches[0].shape[0]
    kv_p = [None for _ in ATTN_GROUPS]
    kv_s = [None for _ in ATTN_GROUPS]
    pool_p, pool_s, sconv_p, sconv_s, ffn_p, ffn_s = [], [], [], [], [], []
    ia = ib = ic = 0
    tm_ffn = 1024
    tm = 512
    for i in range(DEPTH):
        kind = i % N_MIXERS
        g1 = ln_g[i, 0].reshape(1, D_MODEL)
        b1 = ln_b[i, 0].reshape(1, D_MODEL)
        g2 = ln_g[i, 1].reshape(1, D_MODEL)
        b2 = ln_b[i, 1].reshape(1, D_MODEL)
        if kind == 0:
            x3 = xp.reshape(batch, seq, D_MODEL)
            outs_p, lses_p, outs_s, lses_s = [], [], [], []
            for g, (win, dil) in enumerate(ATTN_GROUPS):
                n = seq // dil
                qkv_g = _qkv_rope(x3, w_qkv, ia, g, dil, cos_p[g], sin_p[g], tm_qkv, BF16,
                                  q_scale=ATTN_SCALE * LOG2E)
                o, l = _attn_prompt(qkv_g.reshape(batch * dil, n, 3 * D_MODEL), g)
                outs_p.append(o.reshape(batch, dil, n, D_MODEL))
                lses_p.append(l.reshape(batch, dil, n, LANES))
                kv_p[g] = _kv_tail(x3, w_qkv, cos_p[0], sin_p[0], kv_p[g], ia, n_attn, g, min(win, seq))
                qkv_s = _qkv_rope(xs.reshape(1, nb, D_MODEL), w_qkv, ia, g, 1, cos_s, sin_s, nb, F32)
                kv_s[g], o, l = _cache_step(qkv_s.reshape(nb, 3 * D_MODEL), caches_t[g], kv_s[g], ia, g, dil)
                outs_s.append(o.reshape(nb, D_MODEL))
                lses_s.append(l.reshape(nb, LANES))
            xp = _merge_wo_ln(outs_p, lses_p, x3, w_o[ia], g1, b1, tm).reshape(batch * seq, D_MODEL)
            xs = _merge_step(outs_s, lses_s, xs, w_o[ia], g1, b1)
            ia += 1
        elif kind == 1:
            sc = pool_scale[ib].reshape(1, D_MODEL)
            xp, st = _pool_prompt(xp, p_in[ib], p_grp[ib], sc, p_out[ib], g1, b1, batch, seq, tm)
            pool_p.append(st)
            xs, st = _pool_step(xs, state_pool[ib], p_in[ib], p_grp[ib], sc, p_out[ib], g1, b1)
            pool_s.append(st)
            ib += 1
        else:
            xp, st = _sconv_prompt(xp, s_in[ic], sconv_k[ic], s_out[ic], g1, b1, batch, seq, tm)
            sconv_p.append(st)
            xs, st = _sconv_step(xs, state_sconv[ic], s_in[ic], sconv_k[ic], s_out[ic], g1, b1)
            sconv_s.append(st)
            ic += 1
        xp, st = _ffn_prompt(xp, f_up, ffn_k[i], f_down, i, g2, b2, batch, seq, tm_ffn)
        ffn_p.append(st)
        xs, st = _ffn_step(xs, state_ffn_conv[i], f_up, ffn_k[i], f_down, i, g2, b2)
        ffn_s.append(st)

    y_prompt = xp.reshape(batch, seq, D_MODEL)
    y_sample = xs.reshape(nb, 1, D_MODEL)

    def time_major(t, rows):
        t = t.reshape(n_attn, rows, 2, N_HEADS, HEAD_DIM, t.shape[-1])
        return t.transpose(0, 1, 5, 2, 3, 4)

    return (y_prompt, y_sample,
            time_major(kv_p[0], batch), time_major(kv_s[0], nb),
            time_major(kv_p[1], batch), time_major(kv_s[1], nb),
            time_major(kv_p[2], batch), time_major(kv_s[2], nb),
            jnp.stack(pool_p), jnp.stack(pool_s),
            jnp.stack(sconv_p), jnp.stack(sconv_s),
            jnp.stack(ffn_p), jnp.stack(ffn_s))
```
